```python
import jax, jax.numpy as jnp
from jax import lax
import numpy as np

D_MODEL = 1024
BATCH = 2
SEQ = 8192
DEPTH = 2
DEC_BATCH = 32
DEC_SEQ = 16
PAST_LEN = 1024

CHUNK = 64
N_META = 16
POOL_WIDTH = D_MODEL // 2
POOL_WINDOWS = (2, 4, 8, 16)
POOL_GROUPS = len(POOL_WINDOWS)
POOL_GROUP_DIM = POOL_WIDTH // POOL_GROUPS
POOL_STATE = max(POOL_WINDOWS) - 1
SB_HEADS = 8
SB_HEAD_DIM = 64
SB_WIDTH = SB_HEADS * SB_HEAD_DIM
MIX_WIDTH = POOL_WIDTH + SB_WIDTH
IN_WIDTH = POOL_WIDTH + 3 * SB_WIDTH
Q_BLOCK = 128
D_FF = 2816
N_EXPERTS = 8
TOP_K = 2
D_EXPERT = 3584
N_DENSE = (DEPTH + 1) // 2
N_MOE = DEPTH // 2
EPS = 1e-6

kernel_name = "hymba_pool_stickbreak_streaming_step"


def rmsnorm(x, g):
    x32 = x.astype(jnp.float32)
    y = x32 * lax.rsqrt(jnp.mean(x32 * x32, axis=-1, keepdims=True) + EPS)
    return (y * g.astype(jnp.float32)).astype(x.dtype)


def split_inputs(u):
    b, t, _ = u.shape
    pv = u[..., :POOL_WIDTH]
    q = u[..., POOL_WIDTH:POOL_WIDTH + SB_WIDTH].reshape(b, t, SB_HEADS, SB_HEAD_DIM)
    k = u[..., POOL_WIDTH + SB_WIDTH:POOL_WIDTH + 2 * SB_WIDTH].reshape(b, t, SB_HEADS, SB_HEAD_DIM)
    v = u[..., POOL_WIDTH + 2 * SB_WIDTH:].reshape(b, t, SB_HEADS, SB_HEAD_DIM)
    return pv, q, k, v


def pool_mix(buf, n_prefix, w, scale):
    bsz, lb, _ = buf.shape
    b32 = buf.astype(jnp.float32)
    c = jnp.concatenate([jnp.zeros((bsz, 1, POOL_WIDTH), jnp.float32),
                         jnp.cumsum(b32, axis=1)], axis=1)
    idx = jnp.arange(n_prefix, lb)
    upper = c[:, idx + 1]
    xs = b32[:, n_prefix:]
    diffs = []
    for g, win in enumerate(POOL_WINDOWS):
        sl = slice(g * POOL_GROUP_DIM, (g + 1) * POOL_GROUP_DIM)
        lo = c[:, jnp.maximum(idx + 1 - win, 0), sl]
        cnt = jnp.minimum(idx + 1, win).astype(jnp.float32)[None, :, None]
        diffs.append((upper[..., sl] - lo) / cnt - xs[..., sl])
    d = jnp.stack(diffs, axis=2).astype(buf.dtype)
    y = jnp.einsum('btgc,gcd->btgd', d, w).reshape(bsz, -1, POOL_WIDTH)
    return y * scale


def sb_attend(q, k, v, q_pos, k_pos):
    z = jnp.einsum('bqhd,bkhd->bhqk', q, k).astype(jnp.float32) * (SB_HEAD_DIM ** -0.5)
    mask = k_pos[None, :] < q_pos[:, None]
    log_beta = jax.nn.log_sigmoid(z)
    log_1m = jnp.where(mask, jax.nn.log_sigmoid(-z), 0.0)
    rest = lax.cumsum(log_1m, axis=3, reverse=True) - log_1m
    a = jnp.where(mask, jnp.exp(log_beta + rest), 0.0)
    return jnp.einsum('bhqk,bkhd->bqhd', a.astype(v.dtype), v)


def sb_prompt(q, k, v):
    length = q.shape[1]
    outs = []
    for start in range(0, length, Q_BLOCK):
        end = min(start + Q_BLOCK, length)
        outs.append(sb_attend(q[:, start:end], k[:, :end], v[:, :end],
                              jnp.arange(start, end), jnp.arange(end)))
    return jnp.concatenate(outs, axis=1)


def swiglu(x, wg, wu, wd):
    return (jax.nn.silu(x @ wg) * (x @ wu)) @ wd


def moe_swiglu(x, router, wg, wu, wd):
    logits = jnp.einsum('btd,de->bte', x, router).astype(jnp.float32)
    top_v, top_i = lax.top_k(logits, TOP_K)
    gates = jax.nn.softmax(top_v, axis=-1)
    dense_gate = jnp.sum(jax.nn.one_hot(top_i, N_EXPERTS, dtype=jnp.float32) * gates[..., None], axis=-2)
    out = jnp.zeros_like(x)
    for e in range(N_EXPERTS):
        out = out + dense_gate[..., e:e + 1].astype(x.dtype) * swiglu(x, wg[e], wu[e], wd[e])
    return out


def channel_mix(l, h, norm_ffn, ffn_w_gate, ffn_w_up, ffn_w_down,
                moe_router, moe_w_gate, moe_w_up, moe_w_down):
    hn = rmsnorm(h, norm_ffn[l])
    j = l // 2
    if l % 2 == 0:
        return swiglu(hn, ffn_w_gate[j], ffn_w_up[j], ffn_w_down[j])
    return moe_swiglu(hn, moe_router[j], moe_w_gate[j], moe_w_up[j], moe_w_down[j])


def setup_inputs(seed: int = 0) -> dict:
    key = jax.random.key(seed)
    ks = jax.random.split(key, 24)
    f32 = jnp.float32
    nrm = lambda k, s, sc: jax.random.normal(k, s, f32) * sc
    return {
        'x_prompt': nrm(ks[0], (BATCH, SEQ, D_MODEL), 1.0),
        'x_sample': nrm(ks[1], (DEC_BATCH, DEC_SEQ, D_MODEL), 1.0),
        'cache_k': nrm(ks[2], (DEPTH, DEC_BATCH, PAST_LEN, SB_HEADS, SB_HEAD_DIM), 1.0),
        'cache_v': nrm(ks[3], (DEPTH, DEC_BATCH, PAST_LEN, SB_HEADS, SB_HEAD_DIM), 1.0),
        'state_pool': nrm(ks[4], (DEPTH, DEC_BATCH, POOL_STATE, POOL_WIDTH), 1.0),
        'meta_tokens': nrm(ks[5], (N_META, D_MODEL), 1.0),
        'norm_mix': 1.0 + nrm(ks[6], (DEPTH, D_MODEL), 0.02),
        'w_in': nrm(ks[7], (DEPTH, D_MODEL, IN_WIDTH), D_MODEL ** -0.5),
        'pool_w': nrm(ks[8], (DEPTH, POOL_GROUPS, POOL_GROUP_DIM, POOL_GROUP_DIM), POOL_GROUP_DIM ** -0.5),
        'pool_scale': 1.0 + nrm(ks[9], (DEPTH, POOL_WIDTH), 0.02),
        'w_out': nrm(ks[10], (DEPTH, MIX_WIDTH, D_MODEL), MIX_WIDTH ** -0.5),
        'norm_ffn': 1.0 + nrm(ks[11], (DEPTH, D_MODEL), 0.02),
        'ffn_w_gate': nrm(ks[12], (N_DENSE, D_MODEL, D_FF), D_MODEL ** -0.5),
        'ffn_w_up': nrm(ks[13], (N_DENSE, D_MODEL, D_FF), D_MODEL ** -0.5),
        'ffn_w_down': nrm(ks[14], (N_DENSE, D_FF, D_MODEL), D_FF ** -0.5),
        'moe_router': nrm(ks[15], (N_MOE, D_MODEL, N_EXPERTS), D_MODEL ** -0.5),
        'moe_w_gate': nrm(ks[16], (N_MOE, N_EXPERTS, D_MODEL, D_EXPERT), D_MODEL ** -0.5),
        'moe_w_up': nrm(ks[17], (N_MOE, N_EXPERTS, D_MODEL, D_EXPERT), D_MODEL ** -0.5),
        'moe_w_down': nrm(ks[18], (N_MOE, N_EXPERTS, D_EXPERT, D_MODEL), D_EXPERT ** -0.5),
        'final_norm': 1.0 + nrm(ks[19], (D_MODEL,), 0.02),
    }


def reference(x_prompt, x_sample, cache_k, cache_v, state_pool, meta_tokens, norm_mix, w_in,
              pool_w, pool_scale, w_out, norm_ffn, ffn_w_gate, ffn_w_up, ffn_w_down,
              moe_router, moe_w_gate, moe_w_up, moe_w_down, final_norm):
    bsz = x_prompt.shape[0]
    meta = jnp.broadcast_to(meta_tokens[None].astype(x_prompt.dtype), (bsz, N_META, D_MODEL))
    h = jnp.concatenate([meta, x_prompt], axis=1)
    length = h.shape[1]
    p_k, p_v, p_pool = [], [], []
    for l in range(DEPTH):
        hn = rmsnorm(h, norm_mix[l])
        pv, q, k, v = split_inputs(hn @ w_in[l])
        pool_out = pool_mix(pv, 0, pool_w[l], pool_scale[l])
        sb_out = sb_prompt(q, k, v).reshape(bsz, length, SB_WIDTH)
        h = h + jnp.concatenate([pool_out, sb_out], axis=-1) @ w_out[l]
        h = h + channel_mix(l, h, norm_ffn, ffn_w_gate, ffn_w_up, ffn_w_down,
                            moe_router, moe_w_gate, moe_w_up, moe_w_down)
        p_k.append(k)
        p_v.append(v)
        p_pool.append(pv[:, -POOL_STATE:])
    y_prompt = rmsnorm(h, final_norm)[:, N_META:]

    dbsz, tnew = x_sample.shape[0], x_sample.shape[1]
    past = cache_k.shape[2]
    q_pos = past + jnp.arange(tnew)
    k_pos = jnp.arange(past + tnew)
    h = x_sample
    s_k, s_v, s_pool = [], [], []
    for l in range(DEPTH):
        hn = rmsnorm(h, norm_mix[l])
        pv, q, k, v = split_inputs(hn @ w_in[l])
        pbuf = jnp.concatenate([state_pool[l].astype(pv.dtype), pv], axis=1)
        pool_out = pool_mix(pbuf, POOL_STATE, pool_w[l], pool_scale[l])
        kk = jnp.concatenate([cache_k[l].astype(k.dtype), k], axis=1)
        vv = jnp.concatenate([cache_v[l].astype(v.dtype), v], axis=1)
        sb_out = sb_attend(q, kk, vv, q_pos, k_pos).reshape(dbsz, tnew, SB_WIDTH)
        h = h + jnp.concatenate([pool_out, sb_out], axis=-1) @ w_out[l]
        h = h + channel_mix(l, h, norm_ffn, ffn_w_gate, ffn_w_up, ffn_w_down,
                            moe_router, moe_w_gate, moe_w_up, moe_w_down)
        s_k.append(k)
        s_v.append(v)
        s_pool.append(pbuf[:, -POOL_STATE:])
    y_sample = rmsnorm(h, final_norm)

    return (y_prompt, y_sample, jnp.stack(p_k), jnp.stack(p_v), jnp.stack(p_pool),
            jnp.stack(s_k), jnp.stack(s_v), jnp.stack(s_pool))
```

```python
import functools

import jax
import jax.numpy as jnp
from jax import lax
from jax.experimental import pallas as pl
from jax.experimental.pallas import tpu as pltpu

F32 = jnp.float32
BF16 = jnp.bfloat16

D_MODEL = 1024
BATCH = 2
SEQ = 8192
DEPTH = 2
DEC_BATCH = 32
DEC_SEQ = 16
PAST_LEN = 1024
N_META = 16
POOL_WIDTH = 512
POOL_WINDOWS = (2, 4, 8, 16)
POOL_GROUP_DIM = 128
POOL_STATE = 15
SB_HEADS = 8
SB_HEAD_DIM = 64
SB_WIDTH = 512
IN_WIDTH = 2048
D_FF = 2816
N_EXPERTS = 8
D_EXPERT = 3584
EPS = 1e-6

LANES = 128
SUBLANES = 8
TILE = 256
SEG = TILE // SUBLANES
FRONT = TILE - N_META
SEQ_PAD = FRONT + N_META + SEQ
SEQ_TILES = SEQ_PAD // TILE
ROWS_P = BATCH * SEQ_PAD
ROWS_S = DEC_BATCH * DEC_SEQ
ROWS = ROWS_P + ROWS_S
TM = 512
HALO = 16
VMEM_LIMIT = 56 * 1024 * 1024


def _params(sem, vmem=None):
    return pltpu.CompilerParams(dimension_semantics=sem, vmem_limit_bytes=vmem)


def _rms(x, gain):
    ms = jnp.mean(x * x, axis=-1, keepdims=True)
    return x * lax.rsqrt(ms + EPS) * gain


def _inproj_kernel(x_ref, g_ref, w_ref, pv_ref, q_ref, k_ref, v_ref):
    hn = _rms(x_ref[...], g_ref[...]).astype(BF16)
    pv_ref[...] = jnp.dot(hn, w_ref[:, 0:512], preferred_element_type=F32)
    q = jnp.dot(hn, w_ref[:, 512:1024], preferred_element_type=F32)
    q_ref[...] = (q * (SB_HEAD_DIM ** -0.5)).astype(BF16)
    k_ref[...] = jnp.dot(hn, w_ref[:, 1024:1536], preferred_element_type=F32)
    v_ref[...] = jnp.dot(hn, w_ref[:, 1536:2048], preferred_element_type=F32)


def _inproj(h, gain, w):
    row = lambda i: (i, 0)
    fixed = lambda i: (0, 0)
    return pl.pallas_call(
        _inproj_kernel,
        grid=(ROWS // TM,),
        in_specs=[pl.BlockSpec((TM, D_MODEL), row),
                  pl.BlockSpec((1, D_MODEL), fixed),
                  pl.BlockSpec((D_MODEL, IN_WIDTH), fixed)],
        out_specs=[pl.BlockSpec((TM, 512), row)] * 4,
        out_shape=[jax.ShapeDtypeStruct((ROWS, 512), F32),
                   jax.ShapeDtypeStruct((ROWS, 512), BF16),
                   jax.ShapeDtypeStruct((ROWS, 512), F32),
                   jax.ShapeDtypeStruct((ROWS, 512), F32)],
        compiler_params=_params(("arbitrary",), VMEM_LIMIT),
        name="inproj",
    )(h, gain, w)


def _sb_tile(z, carry, masked):
    e = jnp.exp(z)
    om = 1.0 / (1.0 + e)
    beta = 1.0 - om
    if masked:
        r = lax.broadcasted_iota(jnp.int32, (TILE, TILE), 0)
        c = lax.broadcasted_iota(jnp.int32, (TILE, TILE), 1)
        key = ((r & (SUBLANES - 1)) * SEG) + (r >> 3)
        valid = key < c
        om = jnp.where(valid, om, 1.0)
        beta = jnp.where(valid, beta, 0.0)
    run = jnp.ones((SUBLANES, TILE), F32)
    parts = [None] * SEG
    for a in reversed(range(SEG)):
        sl = slice(a * SUBLANES, (a + 1) * SUBLANES)
        parts[a] = beta[sl] * run
        run = run * om[sl]
    sub = lax.broadcasted_iota(jnp.int32, (SUBLANES, TILE), 0)
    y = run
    for k in (1, 2, 4):
        y = y * jnp.where(sub + k < SUBLANES, pltpu.roll(y, SUBLANES - k, 0), 1.0)
    off = carry * jnp.where(sub + 1 < SUBLANES, pltpu.roll(y, SUBLANES - 1, 0), 1.0)
    new_carry = carry * jnp.broadcast_to(y[0:1, :], (SUBLANES, TILE))
    a_t = jnp.concatenate([p * off for p in parts], axis=0).astype(BF16)
    return a_t, new_carry


def _attn_prompt_kernel(k_ref, qt_ref, vt_ref, o_ref):
    half = SB_HEAD_DIM

    def q_tile(i, _):
        qa = qt_ref[i, 0]
        qb = qt_ref[i, 1]

        def kv_tile(j, masked, st):
            ca, cb, acca, accb = st
            kk = k_ref[j]
            za = jnp.dot(kk, qa, preferred_element_type=F32)
            zb = jnp.dot(kk, qb, preferred_element_type=F32)
            aa, ca = _sb_tile(za, ca, masked)
            ab, cb = _sb_tile(zb, cb, masked)
            acca = acca + jnp.dot(vt_ref[j, 0:half, :], aa, preferred_element_type=F32)
            accb = accb + jnp.dot(vt_ref[j, half:2 * half, :], ab, preferred_element_type=F32)
            return ca, cb, acca, accb

        ones = jnp.ones((SUBLANES, TILE), F32)
        zeros = jnp.zeros((half, TILE), F32)
        st = kv_tile(i, True, (ones, ones, zeros, zeros))
        st = lax.fori_loop(0, i, lambda t, s: kv_tile(i - 1 - t, False, s), st)
        o_ref[i] = jnp.concatenate([st[2], st[3]], axis=0)
        return 0

    lax.fori_loop(0, SEQ_TILES, q_tile, 0)


def _attn_prompt(k4, qt6, vt4):
    blk = lambda *shape: pl.BlockSpec((None, None) + shape, lambda p, g: (p, g) + (0,) * len(shape))
    return pl.pallas_call(
        _attn_prompt_kernel,
        grid=(BATCH, SB_HEADS // 2),
        in_specs=[blk(SEQ_TILES, TILE, LANES),
                  blk(SEQ_TILES, 2, LANES, TILE),
                  blk(SEQ_TILES, LANES, TILE)],
        out_specs=blk(SEQ_TILES, LANES, TILE),
        out_shape=jax.ShapeDtypeStruct((BATCH, SB_HEADS // 2, SEQ_TILES, LANES, TILE), F32),
        compiler_params=_params(("arbitrary", "arbitrary"), VMEM_LIMIT),
        name="attn_prompt",
    )(k4, qt6, vt4)


def _rev_excl_cumprod(om, scr):
    n = om.shape[0]
    sub = lax.broadcasted_iota(jnp.int32, (n, LANES), 0) & (SUBLANES - 1)
    scr[n:n + SUBLANES, :] = jnp.ones((SUBLANES, LANES), F32)
    y = om
    for k in (1, 2, 4):
        scr[0:n, :] = y
        y = y * jnp.where(sub + k < SUBLANES, scr[k:n + k, :], 1.0)
    scr[0:n, :] = y
    ex = jnp.where(sub + 1 < SUBLANES, scr[1:n + 1, :], 1.0)
    run = jnp.ones((SUBLANES, LANES), F32)
    out = [None] * (n // SUBLANES)
    for a in reversed(range(n // SUBLANES)):
        sl = slice(a * SUBLANES, (a + 1) * SUBLANES)
        out[a] = ex[sl] * run
        run = run * jnp.broadcast_to(y[a * SUBLANES:a * SUBLANES + 1, :], (SUBLANES, LANES))
    return jnp.concatenate(out, axis=0)


def _attn_sample_kernel(ck_ref, cv_ref, kn_ref, vn_ref, qbd_ref, o_ref, scr):
    qbd = qbd_ref[0]
    pad = jnp.zeros((LANES - DEC_SEQ, SB_WIDTH), F32)
    kn = jnp.concatenate([kn_ref[...], pad], axis=0).astype(BF16)
    vn = jnp.concatenate([vn_ref[...], pad], axis=0).astype(BF16)
    zc = jnp.dot(ck_ref[0].astype(BF16), qbd, preferred_element_type=F32)
    zn = jnp.dot(kn, qbd, preferred_element_type=F32)
    z = jnp.concatenate([zc, zn], axis=0)
    n = PAST_LEN + LANES
    r = lax.broadcasted_iota(jnp.int32, (n, LANES), 0)
    c = lax.broadcasted_iota(jnp.int32, (n, LANES), 1)
    valid = (r < PAST_LEN) | ((r - PAST_LEN) < (c & (DEC_SEQ - 1)))
    e = jnp.exp(z)
    om = 1.0 / (1.0 + e)
    beta = jnp.where(valid, 1.0 - om, 0.0)
    om = jnp.where(valid, om, 1.0)
    a_t = beta * _rev_excl_cumprod(om, scr)
    ac = a_t[0:PAST_LEN].T.astype(BF16)
    an = a_t[PAST_LEN:n].T.astype(BF16)
    p = jnp.dot(ac, cv_ref[0].astype(BF16), preferred_element_type=F32)
    p = p + jnp.dot(an, vn, preferred_element_type=F32)
    pr = lax.broadcasted_iota(jnp.int32, (LANES, SB_WIDTH), 0)
    pc = lax.broadcasted_iota(jnp.int32, (LANES, SB_WIDTH), 1)
    p = jnp.where((pr >> 4) == (pc >> 6), p, 0.0)
    out = p[0:DEC_SEQ]
    for h in range(1, SB_HEADS):
        out = out + p[h * DEC_SEQ:(h + 1) * DEC_SEQ]
    o_ref[...] = out.astype(BF16)


def _attn_sample(ck, cv, k, v, qbd):
    new = lambda s: (ROWS_P // DEC_SEQ + s, 0)
    per = lambda s: (s, 0, 0)
    return pl.pallas_call(
        _attn_sample_kernel,
        grid=(DEC_BATCH,),
        in_specs=[pl.BlockSpec((1, PAST_LEN, SB_WIDTH), per),
                  pl.BlockSpec((1, PAST_LEN, SB_WIDTH), per),
                  pl.BlockSpec((DEC_SEQ, SB_WIDTH), new),
                  pl.BlockSpec((DEC_SEQ, SB_WIDTH), new),
                  pl.BlockSpec((1, SB_WIDTH, LANES), per)],
        out_specs=pl.BlockSpec((DEC_SEQ, SB_WIDTH), lambda s: (s, 0)),
        out_shape=jax.ShapeDtypeStruct((ROWS_S, SB_WIDTH), BF16),
        scratch_shapes=[pltpu.VMEM((PAST_LEN + LANES + SUBLANES, LANES), F32)],
        compiler_params=_params(("arbitrary",), VMEM_LIMIT),
        name="attn_sample",
    )(ck, cv, k, v, qbd)


def _pool_kernel(*refs, tm, use_halo):
    if use_halo:
        pv_ref, halo_ref, w_ref, sc_ref, o_ref, s1, s2, s4, s8 = refs
    else:
        pv_ref, w_ref, sc_ref, o_ref, s1, s2, s4, s8 = refs
    i = pl.program_id(0)
    off = SUBLANES
    n = HALO + tm
    x = pv_ref[...]
    zero8 = jnp.zeros((off, POOL_WIDTH), F32)
    if use_halo:
        first = (i % (SEQ_PAD // tm)) == 0
        halo = jnp.where(first, 0.0, halo_ref[...])
    else:
        halo = jnp.zeros((HALO, POOL_WIDTH), F32)
    s1[0:off, :] = zero8
    s1[off:off + HALO, :] = halo
    s1[off + HALO:off + n, :] = x
    g = POOL_GROUP_DIM
    t2 = s1[off:off + n, :] + s1[off - 1:off - 1 + n, :]
    s2[0:off, :] = zero8[:, g:]
    s2[off:off + n, :] = t2[:, g:]
    t4 = t2[:, g:] + s2[off - 2:off - 2 + n, :]
    s4[0:off, :] = zero8[:, 2 * g:]
    s4[off:off + n, :] = t4[:, g:]
    t8 = t4[:, g:] + s4[off - 4:off - 4 + n, :]
    s8[0:off, :] = zero8[:, 3 * g:]
    s8[off:off + n, :] = t8[:, g:]
    t16 = t8[:, g:] + s8[off - 8:off - 8 + n, :]
    sums = (t2[HALO:, 0:g], t4[HALO:, 0:g], t8[HALO:, 0:g], t16[HALO:, :])
    if use_halo:
        idx = (i % (SEQ_PAD // tm)) * tm - FRONT + lax.broadcasted_iota(jnp.int32, (tm, g), 0)
    for gi, win in enumerate(POOL_WINDOWS):
        if use_halo:
            cnt = jnp.clip(idx + 1, 1, win).astype(F32)
        else:
            cnt = jnp.full((tm, g), float(win), F32)
        d = sums[gi] / cnt - x[:, gi * g:(gi + 1) * g]
        y = jnp.dot(d.astype(BF16), w_ref[gi], preferred_element_type=F32)
        o_ref[:, gi * g:(gi + 1) * g] = (y * sc_ref[:, gi * g:(gi + 1) * g]).astype(BF16)


def _pool(pv, w, scale, *, rows, tm, use_halo, row_offset_tiles=0):
    g = POOL_GROUP_DIM
    row = lambda i: (i + row_offset_tiles, 0)
    in_specs = [pl.BlockSpec((tm, POOL_WIDTH), row)]
    args = [pv]
    if use_halo:
        in_specs.append(pl.BlockSpec((HALO, POOL_WIDTH),
                                     lambda i: (jnp.maximum(i * (tm // HALO) - 1, 0), 0)))
        args.append(pv)
    in_specs += [pl.BlockSpec((4, g, g), lambda i: (0, 0, 0)),
                 pl.BlockSpec((1, POOL_WIDTH), lambda i: (0, 0))]
    args += [w, scale]
    n = SUBLANES + HALO + tm
    return pl.pallas_call(
        functools.partial(_pool_kernel, tm=tm, use_halo=use_halo),
        grid=(rows // tm,),
        in_specs=in_specs,
        out_specs=pl.BlockSpec((tm, POOL_WIDTH), lambda i: (i, 0)),
        out_shape=jax.ShapeDtypeStruct((rows, POOL_WIDTH), BF16),
        scratch_shapes=[pltpu.VMEM((n, 4 * g), F32), pltpu.VMEM((n, 3 * g), F32),
                        pltpu.VMEM((n, 2 * g), F32), pltpu.VMEM((n, g), F32)],
        compiler_params=_params(("arbitrary",), VMEM_LIMIT),
        name="pool_prompt" if use_halo else "pool_sample",
    )(*args)


def _mix_kernel(*refs, route):
    if route:
        h_ref, pool_ref, sb_ref, w_ref, g_ref, r_ref, hmid_ref, hn_ref, dg_ref = refs
    else:
        h_ref, pool_ref, sb_ref, w_ref, g_ref, hmid_ref, hn_ref = refs
    hmid = (h_ref[...]
            + jnp.dot(pool_ref[...], w_ref[0:POOL_WIDTH, :], preferred_element_type=F32)
            + jnp.dot(sb_ref[...], w_ref[POOL_WIDTH:, :], preferred_element_type=F32))
    hmid_ref[...] = hmid
    hn = _rms(hmid, g_ref[...])
    hn_ref[...] = hn.astype(BF16)
    if route:
        tm = hn.shape[0]
        logits = jnp.dot(hn, r_ref[...], preferred_element_type=F32,
                         precision=lax.Precision.HIGHEST)
        lane = lax.broadcasted_iota(jnp.int32, (tm, LANES), 1)
        neg = jnp.float32(-jnp.inf)
        logits = jnp.where(lane < N_EXPERTS, logits, neg)
        m1 = jnp.max(logits, axis=-1, keepdims=True)
        i1 = jnp.min(jnp.where(logits == m1, lane, LANES), axis=-1, keepdims=True)
        rest = jnp.where(lane == i1, neg, logits)
        m2 = jnp.max(rest, axis=-1, keepdims=True)
        i2 = jnp.min(jnp.where(rest == m2, lane, LANES), axis=-1, keepdims=True)
        t = jnp.exp(m2 - m1)
        g1 = 1.0 / (1.0 + t)
        g2 = t / (1.0 + t)
        dg_ref[...] = jnp.where(lane == i1, g1, 0.0) + jnp.where(lane == i2, g2, 0.0)


def _mix(h, pool, sb, w, gain, router=None):
    row = lambda i: (i, 0)
    fixed = lambda i: (0, 0)
    route = router is not None
    in_specs = [pl.BlockSpec((TM, D_MODEL), row),
                pl.BlockSpec((TM, POOL_WIDTH), row),
                pl.BlockSpec((TM, SB_WIDTH), row),
                pl.BlockSpec((D_MODEL, D_MODEL), fixed),
                pl.BlockSpec((1, D_MODEL), fixed)]
    out_specs = [pl.BlockSpec((TM, D_MODEL), row), pl.BlockSpec((TM, D_MODEL), row)]
    out_shape = [jax.ShapeDtypeStruct((ROWS, D_MODEL), F32),
                 jax.ShapeDtypeStruct((ROWS, D_MODEL), BF16)]
    args = [h, pool, sb, w, gain]
    if route:
        in_specs.append(pl.BlockSpec((D_MODEL, LANES), fixed))
        out_specs.append(pl.BlockSpec((TM, LANES), row))
        out_shape.append(jax.ShapeDtypeStruct((ROWS, LANES), F32))
        args.append(router)
    return pl.pallas_call(
        functools.partial(_mix_kernel, route=route),
        grid=(ROWS // TM,),
        in_specs=in_specs, out_specs=out_specs, out_shape=out_shape,
        compiler_params=_params(("arbitrary",), VMEM_LIMIT),
        name="mix_route" if route else "mix",
    )(*args)


def _silu_mul(g, u):
    return g * (1.0 / (1.0 + jnp.exp(-g))) * u


FF_CHUNK = 256


def _ffn_kernel(hmid_ref, hn_ref, wg_ref, wu_ref, wd_ref, o_ref, act):
    hn = hn_ref[...]
    for c in range(D_FF // FF_CHUNK):
        sl = slice(c * FF_CHUNK, (c + 1) * FF_CHUNK)
        g = jnp.dot(hn, wg_ref[:, sl], preferred_element_type=F32)
        u = jnp.dot(hn, wu_ref[:, sl], preferred_element_type=F32)
        act[:, sl] = _silu_mul(g, u).astype(BF16)
    o_ref[...] = hmid_ref[...] + jnp.dot(act[...], wd_ref[...], preferred_element_type=F32)


def _ffn(hmid, hn, wg, wu, wd):
    row = lambda i: (i, 0)
    fixed = lambda i: (0, 0)
    once = pl.Buffered(1)
    return pl.pallas_call(
        _ffn_kernel,
        grid=(ROWS // TM,),
        in_specs=[pl.BlockSpec((TM, D_MODEL), row),
                  pl.BlockSpec((TM, D_MODEL), row),
                  pl.BlockSpec((D_MODEL, D_FF), fixed, pipeline_mode=once),
                  pl.BlockSpec((D_MODEL, D_FF), fixed, pipeline_mode=once),
                  pl.BlockSpec((D_FF, D_MODEL), fixed, pipeline_mode=once)],
        out_specs=pl.BlockSpec((TM, D_MODEL), row),
        out_shape=jax.ShapeDtypeStruct((ROWS, D_MODEL), F32),
        scratch_shapes=[pltpu.VMEM((TM, D_FF), BF16)],
        compiler_params=_params(("arbitrary",), VMEM_LIMIT),
        name="ffn",
    )(hmid, hn, wg, wu, wd)


EX_CHUNK = 512


def _moe_kernel(hmid_ref, hn_ref, dg_ref, wg_ref, wu_ref, wd_ref, fg_ref, y_ref, acc):
    e = pl.program_id(1)
    c = pl.program_id(2)

    @pl.when((e == 0) & (c == 0))
    def _():
        acc[...] = hmid_ref[...]

    hn = hn_ref[...]
    g = jnp.dot(hn, wg_ref[0], preferred_element_type=F32)
    u = jnp.dot(hn, wu_ref[0], preferred_element_type=F32)
    a = _silu_mul(g, u).astype(BF16)
    y = jnp.dot(a, wd_ref[0], preferred_element_type=F32)
    lane = lax.broadcasted_iota(jnp.int32, dg_ref.shape, 1)
    gate = jnp.sum(jnp.where(lane == e, dg_ref[...], 0.0), axis=-1, keepdims=True)
    acc[...] += gate * y

    @pl.when((e == N_EXPERTS - 1) & (c == D_EXPERT // EX_CHUNK - 1))
    def _():
        y_ref[...] = _rms(acc[...], fg_ref[...])


def _moe(hmid, hn, dgate, wg, wu, wd, fgain):
    row = lambda i, e, c: (i, 0)
    return pl.pallas_call(
        _moe_kernel,
        grid=(ROWS // TM, N_EXPERTS, D_EXPERT // EX_CHUNK),
        in_specs=[pl.BlockSpec((TM, D_MODEL), row),
                  pl.BlockSpec((TM, D_MODEL), row),
                  pl.BlockSpec((TM, LANES), row),
                  pl.BlockSpec((1, D_MODEL, EX_CHUNK), lambda i, e, c: (e, 0, c)),
                  pl.BlockSpec((1, D_MODEL, EX_CHUNK), lambda i, e, c: (e, 0, c)),
                  pl.BlockSpec((1, EX_CHUNK, D_MODEL), lambda i, e, c: (e, c, 0)),
                  pl.BlockSpec((1, D_MODEL), lambda i, e, c: (0, 0))],
        out_specs=pl.BlockSpec((TM, D_MODEL), row),
        out_shape=jax.ShapeDtypeStruct((ROWS, D_MODEL), F32),
        scratch_shapes=[pltpu.VMEM((TM, D_MODEL), F32)],
        compiler_params=_params(("arbitrary", "arbitrary", "arbitrary"), VMEM_LIMIT),
        name="moe",
    )(hmid, hn, dgate, wg, wu, wd, fgain)


def _prompt_attention_operands(q, k, v):
    pairs = SB_HEADS // 2
    kp = k[:ROWS_P].astype(BF16).reshape(BATCH, SEQ_TILES, SUBLANES, SEG, pairs, LANES)
    k4 = kp.transpose(0, 4, 1, 3, 2, 5).reshape(BATCH, pairs, SEQ_TILES, TILE, LANES)
    vp = v[:ROWS_P].astype(BF16).reshape(BATCH, SEQ_TILES, SUBLANES, SEG, pairs, LANES)
    vt4 = vp.transpose(0, 4, 1, 5, 3, 2).reshape(BATCH, pairs, SEQ_TILES, LANES, TILE)
    qp = q[:ROWS_P].reshape(BATCH, SEQ_TILES, TILE, pairs, 2, SB_HEAD_DIM)
    qt = qp.transpose(0, 3, 1, 4, 5, 2)
    zero = jnp.zeros_like(qt[:, :, :, 0])
    qa = jnp.concatenate([qt[:, :, :, 0], zero], axis=-2)
    qb = jnp.concatenate([zero, qt[:, :, :, 1]], axis=-2)
    qt6 = jnp.stack([qa, qb], axis=3)
    return k4, qt6, vt4


def _sample_qbd(q):
    qs = q[ROWS_P:].reshape(DEC_BATCH, DEC_SEQ, SB_HEADS, SB_HEAD_DIM)
    eye = jnp.eye(SB_HEADS, dtype=q.dtype)
    qbd = jnp.einsum('sthd,hg->shdgt', qs, eye)
    return qbd.reshape(DEC_BATCH, SB_WIDTH, SB_HEADS * DEC_SEQ)


def kernel(x_prompt, x_sample, cache_k, cache_v, state_pool, meta_tokens, norm_mix, w_in, pool_w, pool_scale, w_out, norm_ffn, ffn_w_gate, ffn_w_up, ffn_w_down, moe_router, moe_w_gate, moe_w_up, moe_w_down, final_norm):
    front = jnp.zeros((BATCH, FRONT, D_MODEL), F32)
    meta = jnp.broadcast_to(meta_tokens[None], (BATCH, N_META, D_MODEL))
    h = jnp.concatenate([
        jnp.concatenate([front, meta, x_prompt], axis=1).reshape(ROWS_P, D_MODEL),
        x_sample.reshape(ROWS_S, D_MODEL)], axis=0)

    outs = {name: [] for name in ("pk", "pv", "pp", "sk", "sv", "sp")}
    y = None
    for l in range(DEPTH):
        pv, q, k, v = _inproj(h, norm_mix[l][None], w_in[l].astype(BF16))

        k4, qt6, vt4 = _prompt_attention_operands(q, k, v)
        sb_p = _attn_prompt(k4, qt6, vt4)
        sb_p = sb_p.astype(BF16).transpose(0, 2, 4, 1, 3).reshape(ROWS_P, SB_WIDTH)
        sb_s = _attn_sample(cache_k[l].reshape(DEC_BATCH, PAST_LEN, SB_WIDTH),
                            cache_v[l].reshape(DEC_BATCH, PAST_LEN, SB_WIDTH),
                            k, v, _sample_qbd(q))
        sb = jnp.concatenate([sb_p, sb_s], axis=0)

        pw = pool_w[l].astype(BF16)
        ps = pool_scale[l][None]
        pool_p = _pool(pv, pw, ps, rows=ROWS_P, tm=TILE, use_halo=True)
        pv_s = pv[ROWS_P:].reshape(DEC_BATCH, DEC_SEQ, POOL_WIDTH)
        pbuf = jnp.concatenate([jnp.zeros((DEC_BATCH, 1, POOL_WIDTH), F32), state_pool[l], pv_s],
                               axis=1).reshape(DEC_BATCH * 2 * DEC_SEQ, POOL_WIDTH)
        pool_s = _pool(pbuf, pw, ps, rows=DEC_BATCH * 2 * DEC_SEQ, tm=TM, use_halo=False)
        pool_s = pool_s.reshape(DEC_BATCH, 2 * DEC_SEQ, POOL_WIDTH)[:, DEC_SEQ:].reshape(ROWS_S, POOL_WIDTH)
        pool = jnp.concatenate([pool_p, pool_s], axis=0)

        wo = w_out[l].astype(BF16)
        j = l // 2
        if l % 2 == 0:
            hmid, hn = _mix(h, pool, sb, wo, norm_ffn[l][None])
            h = _ffn(hmid, hn, ffn_w_gate[j].astype(BF16), ffn_w_up[j].astype(BF16),
                     ffn_w_down[j].astype(BF16))
        else:
            router = jnp.pad(moe_router[j], ((0, 0), (0, LANES - N_EXPERTS)))
            hmid, hn, dgate = _mix(h, pool, sb, wo, norm_ffn[l][None], router)
            y = _moe(hmid, hn, dgate, moe_w_gate[j].astype(BF16), moe_w_up[j].astype(BF16),
                     moe_w_down[j].astype(BF16), final_norm[None])

        kp = k[:ROWS_P].reshape(BATCH, SEQ_PAD, SB_HEADS, SB_HEAD_DIM)[:, FRONT:]
        vp = v[:ROWS_P].reshape(BATCH, SEQ_PAD, SB_HEADS, SB_HEAD_DIM)[:, FRONT:]
        outs["pk"].append(kp)
        outs["pv"].append(vp)
        outs["pp"].append(pv[:ROWS_P].reshape(BATCH, SEQ_PAD, POOL_WIDTH)[:, -POOL_STATE:])
        outs["sk"].append(k[ROWS_P:].reshape(DEC_BATCH, DEC_SEQ, SB_HEADS, SB_HEAD_DIM))
        outs["sv"].append(v[ROWS_P:].reshape(DEC_BATCH, DEC_SEQ, SB_HEADS, SB_HEAD_DIM))
        outs["sp"].append(pv_s[:, -POOL_STATE:])

    y_prompt = y[:ROWS_P].reshape(BATCH, SEQ_PAD, D_MODEL)[:, FRONT + N_META:]
    y_sample = y[ROWS_P:].reshape(DEC_BATCH, DEC_SEQ, D_MODEL)
    return (y_prompt, y_sample, jnp.stack(outs["pk"]), jnp.stack(outs["pv"]), jnp.stack(outs["pp"]),
            jnp.stack(outs["sk"]), jnp.stack(outs["sv"]), jnp.stack(outs["sp"]))
```

```python
import functools

import jax
import jax.numpy as jnp
from jax import lax
from jax.experimental import pallas as pl
from jax.experimental.pallas import tpu as pltpu

F32 = jnp.float32
BF16 = jnp.bfloat16
I32 = jnp.int32

D_MODEL = 1024
BATCH = 2
SEQ = 8192
DEPTH = 2
DEC_BATCH = 32
DEC_SEQ = 16
PAST_LEN = 1024
N_META = 16
POOL_WIDTH = 512
POOL_WINDOWS = (2, 4, 8, 16)
POOL_GROUP_DIM = 128
POOL_STATE = 15
SB_HEADS = 8
SB_HEAD_DIM = 64
SB_WIDTH = 512
IN_WIDTH = 2048
D_FF = 2816
N_EXPERTS = 8
D_EXPERT = 3584
EPS = 1e-6

LANES = 128
SUBLANES = 8
TILE = 256
SEG = TILE // SUBLANES
FRONT = TILE - N_META
SEQ_PAD = FRONT + N_META + SEQ
SEQ_TILES = SEQ_PAD // TILE
ROWS_P = BATCH * SEQ_PAD
ROWS_S = DEC_BATCH * DEC_SEQ
ROWS = ROWS_P + ROWS_S
TM = 512
HALO = 16
VMEM_LIMIT = 56 * 1024 * 1024

TM_E = 512
PAIRS = 2 * ROWS
P_MAX = PAIRS + N_EXPERTS * TM_E
NT_E = P_MAX // TM_E
SLOT_ROWS = PAIRS + 2 * TM_E
EX_CHUNK = 512


def _params(sem, vmem=VMEM_LIMIT):
    return pltpu.CompilerParams(dimension_semantics=sem, vmem_limit_bytes=vmem)


def _rms(x, gain):
    ms = jnp.mean(x * x, axis=-1, keepdims=True)
    return x * lax.rsqrt(ms + EPS) * gain


def _inproj_kernel(x_ref, g_ref, w_ref, perm_ref, pv_ref, q_ref, k_ref, v_ref, kp_ref, qt_ref, vt_ref):
    hn = _rms(x_ref[...], g_ref[...]).astype(BF16)
    pv_ref[...] = jnp.dot(hn, w_ref[:, 0:512], preferred_element_type=F32)
    q = jnp.dot(hn, w_ref[:, 512:1024], preferred_element_type=F32) * (SB_HEAD_DIM ** -0.5)
    k = jnp.dot(hn, w_ref[:, 1024:1536], preferred_element_type=F32)
    v = jnp.dot(hn, w_ref[:, 1536:2048], preferred_element_type=F32)
    q_ref[...] = q.astype(BF16)
    k_ref[...] = k
    v_ref[...] = v
    perm = perm_ref[...]
    for half in range(TM // TILE):
        sl = slice(half * TILE, (half + 1) * TILE)
        kp_ref[sl, :] = jnp.dot(perm, k[sl].astype(BF16), preferred_element_type=F32).astype(BF16)
        vp = jnp.dot(perm, v[sl].astype(BF16), preferred_element_type=F32)
        vt_ref[half] = vp.T.astype(BF16)
        qt_ref[half] = q[sl].T.astype(BF16)


def _inproj(h, gain, w, perm):
    row = lambda i: (i, 0)
    fixed = lambda i: (0, 0)
    tiles = lambda i: (i, 0, 0)
    per = TM // TILE
    return pl.pallas_call(
        _inproj_kernel,
        grid=(ROWS // TM,),
        in_specs=[pl.BlockSpec((TM, D_MODEL), row),
                  pl.BlockSpec((1, D_MODEL), fixed),
                  pl.BlockSpec((D_MODEL, IN_WIDTH), fixed),
                  pl.BlockSpec((TILE, TILE), fixed)],
        out_specs=[pl.BlockSpec((TM, 512), row)] * 5
        + [pl.BlockSpec((per, SB_WIDTH, TILE), tiles)] * 2,
        out_shape=[jax.ShapeDtypeStruct((ROWS, 512), F32),
                   jax.ShapeDtypeStruct((ROWS, 512), BF16),
                   jax.ShapeDtypeStruct((ROWS, 512), F32),
                   jax.ShapeDtypeStruct((ROWS, 512), F32),
                   jax.ShapeDtypeStruct((ROWS, 512), BF16),
                   jax.ShapeDtypeStruct((ROWS // TILE, SB_WIDTH, TILE), BF16),
                   jax.ShapeDtypeStruct((ROWS // TILE, SB_WIDTH, TILE), BF16)],
        compiler_params=_params(("arbitrary",)),
        name="inproj",
    )(h, gain, w, perm)


def _key_permutation():
    r = jnp.arange(TILE)
    src = (r % SUBLANES) * SEG + r // SUBLANES
    return (src[:, None] == jnp.arange(TILE)[None, :]).astype(BF16)


def _sb_tile(z, carry, masked):
    e = jnp.exp(z)
    om = 1.0 / (1.0 + e)
    beta = 1.0 - om
    if masked:
        r = lax.broadcasted_iota(I32, (TILE, TILE), 0)
        c = lax.broadcasted_iota(I32, (TILE, TILE), 1)
        key = ((r & (SUBLANES - 1)) * SEG) + (r >> 3)
        valid = key < c
        om = jnp.where(valid, om, 1.0)
        beta = jnp.where(valid, beta, 0.0)
    run = jnp.ones((SUBLANES, TILE), F32)
    parts = [None] * SEG
    for a in reversed(range(SEG)):
        sl = slice(a * SUBLANES, (a + 1) * SUBLANES)
        parts[a] = beta[sl] * run
        run = run * om[sl]
    sub = lax.broadcasted_iota(I32, (SUBLANES, TILE), 0)
    y = run
    for k in (1, 2, 4):
        y = y * jnp.where(sub + k < SUBLANES, pltpu.roll(y, SUBLANES - k, 0), 1.0)
    off = carry * jnp.where(sub + 1 < SUBLANES, pltpu.roll(y, SUBLANES - 1, 0), 1.0)
    new_carry = carry * jnp.broadcast_to(y[0:1, :], (SUBLANES, TILE))
    a_t = jnp.concatenate([p * off for p in parts], axis=0).astype(BF16)
    return a_t, new_carry


def _attn_prompt_kernel(k_ref, qt_ref, vt_ref, o_ref, zbuf, abuf, *, n_tiles):
    half = SB_HEAD_DIM
    row = lax.broadcasted_iota(I32, (LANES, TILE), 0)

    def k_tile(j):
        return k_ref[pl.ds(pl.multiple_of(j * TILE, TILE), TILE), :]

    def q_tile(i, _):
        qt = qt_ref[i].astype(F32)
        qh = (jnp.where(row < half, qt, 0.0).astype(BF16),
              jnp.where(row >= half, qt, 0.0).astype(BF16))
        ones = jnp.ones((SUBLANES, TILE), F32)

        kd = k_tile(i)
        carries = []
        for h in range(2):
            z = jnp.dot(kd, qh[h], preferred_element_type=F32)
            a, c = _sb_tile(z, ones, True)
            abuf[0, h] = a
            abuf[1, h] = jnp.zeros((TILE, TILE), BF16)
            carries.append(c)
        k0 = k_tile(jnp.maximum(i - 1, 0))
        k1 = k_tile(jnp.maximum(i - 2, 0))
        for h in range(2):
            zbuf[0, h] = jnp.dot(k0, qh[h], preferred_element_type=F32)
            zbuf[1, h] = jnp.dot(k1, qh[h], preferred_element_type=F32)

        def weighted_values(acc, v0, v1):
            out = []
            for h in range(2):
                rows = slice(h * half, (h + 1) * half)
                out.append(acc[h]
                           + jnp.dot(v0[rows], abuf[0, h], preferred_element_type=F32)
                           + jnp.dot(v1[rows], abuf[1, h], preferred_element_type=F32))
            return out

        def trip(t, st):
            c0, c1, acc0, acc1, p0, p1 = st
            j0 = i - 1 - 2 * t
            j1 = jnp.maximum(j0 - 1, 0)
            acc = weighted_values((acc0, acc1), vt_ref[p0], vt_ref[p1])
            cs = [c0, c1]
            for h in range(2):
                for s in range(2):
                    a, cs[h] = _sb_tile(zbuf[s, h], cs[h], False)
                    abuf[s, h] = a
            n0 = k_tile(jnp.maximum(j0 - 2, 0))
            n1 = k_tile(jnp.maximum(j0 - 3, 0))
            for h in range(2):
                zbuf[0, h] = jnp.dot(n0, qh[h], preferred_element_type=F32)
                zbuf[1, h] = jnp.dot(n1, qh[h], preferred_element_type=F32)
            return cs[0], cs[1], acc[0], acc[1], j0, j1

        zeros = jnp.zeros((half, TILE), F32)
        trips = (i + 1) // 2
        st = lax.fori_loop(0, trips, trip, (carries[0], carries[1], zeros, zeros, i, i))
        keep1 = (i == 0) | ((i & 1) == 0)
        v1 = jnp.where(keep1, vt_ref[st[5]].astype(F32), 0.0).astype(BF16)
        acc = weighted_values((st[2], st[3]), vt_ref[st[4]], v1)
        out_t = jnp.concatenate(acc, axis=0)
        o_ref[pl.ds(pl.multiple_of(i * TILE, TILE), TILE), :] = out_t.T.astype(BF16)
        return 0

    lax.fori_loop(0, n_tiles, q_tile, 0)


def _attn_prompt(kp, qt3, vt3, *, batch=BATCH, n_tiles=SEQ_TILES):
    rows = n_tiles * TILE
    return pl.pallas_call(
        functools.partial(_attn_prompt_kernel, n_tiles=n_tiles),
        grid=(batch, SB_HEADS // 2),
        in_specs=[pl.BlockSpec((rows, LANES), lambda p, g: (p, g)),
                  pl.BlockSpec((n_tiles, LANES, TILE), lambda p, g: (p, g, 0)),
                  pl.BlockSpec((n_tiles, LANES, TILE), lambda p, g: (p, g, 0))],
        out_specs=pl.BlockSpec((rows, LANES), lambda p, g: (p, g)),
        out_shape=jax.ShapeDtypeStruct((batch * rows, SB_WIDTH), BF16),
        scratch_shapes=[pltpu.VMEM((2, 2, TILE, TILE), F32),
                        pltpu.VMEM((2, 2, TILE, TILE), BF16)],
        compiler_params=_params(("arbitrary", "arbitrary")),
        name="attn_prompt",
    )(kp, qt3, vt3)


def _rev_excl_cumprod(om, scr):
    n = om.shape[0]
    sub = lax.broadcasted_iota(I32, (n, LANES), 0) & (SUBLANES - 1)
    scr[n:n + SUBLANES, :] = jnp.ones((SUBLANES, LANES), F32)
    y = om
    for k in (1, 2, 4):
        scr[0:n, :] = y
        y = y * jnp.where(sub + k < SUBLANES, scr[k:n + k, :], 1.0)
    scr[0:n, :] = y
    ex = jnp.where(sub + 1 < SUBLANES, scr[1:n + 1, :], 1.0)
    run = jnp.ones((SUBLANES, LANES), F32)
    out = [None] * (n // SUBLANES)
    for a in reversed(range(n // SUBLANES)):
        sl = slice(a * SUBLANES, (a + 1) * SUBLANES)
        out[a] = ex[sl] * run
        run = run * jnp.broadcast_to(y[a * SUBLANES:a * SUBLANES + 1, :], (SUBLANES, LANES))
    return jnp.concatenate(out, axis=0)


def _attn_sample_kernel(ck_ref, cv_ref, kn_ref, vn_ref, qbd_ref, o_ref, scr):
    qbd = qbd_ref[0]
    pad = jnp.zeros((LANES - DEC_SEQ, SB_WIDTH), F32)
    kn = jnp.concatenate([kn_ref[...], pad], axis=0).astype(BF16)
    vn = jnp.concatenate([vn_ref[...], pad], axis=0).astype(BF16)
    zc = jnp.dot(ck_ref[0].astype(BF16), qbd, preferred_element_type=F32)
    zn = jnp.dot(kn, qbd, preferred_element_type=F32)
    z = jnp.concatenate([zc, zn], axis=0)
    n = PAST_LEN + LANES
    r = lax.broadcasted_iota(I32, (n, LANES), 0)
    c = lax.broadcasted_iota(I32, (n, LANES), 1)
    valid = (r < PAST_LEN) | ((r - PAST_LEN) < (c & (DEC_SEQ - 1)))
    e = jnp.exp(z)
    om = 1.0 / (1.0 + e)
    beta = jnp.where(valid, 1.0 - om, 0.0)
    om = jnp.where(valid, om, 1.0)
    a_t = beta * _rev_excl_cumprod(om, scr)
    ac = a_t[0:PAST_LEN].T.astype(BF16)
    an = a_t[PAST_LEN:n].T.astype(BF16)
    p = jnp.dot(ac, cv_ref[0].astype(BF16), preferred_element_type=F32)
    p = p + jnp.dot(an, vn, preferred_element_type=F32)
    pr = lax.broadcasted_iota(I32, (LANES, SB_WIDTH), 0)
    pc = lax.broadcasted_iota(I32, (LANES, SB_WIDTH), 1)
    p = jnp.where((pr >> 4) == (pc >> 6), p, 0.0)
    out = p[0:DEC_SEQ]
    for h in range(1, SB_HEADS):
        out = out + p[h * DEC_SEQ:(h + 1) * DEC_SEQ]
    o_ref[...] = out.astype(BF16)


def _attn_sample(ck, cv, k, v, qbd):
    new = lambda s: (ROWS_P // DEC_SEQ + s, 0)
    per = lambda s: (s, 0, 0)
    return pl.pallas_call(
        _attn_sample_kernel,
        grid=(DEC_BATCH,),
        in_specs=[pl.BlockSpec((1, PAST_LEN, SB_WIDTH), per),
                  pl.BlockSpec((1, PAST_LEN, SB_WIDTH), per),
                  pl.BlockSpec((DEC_SEQ, SB_WIDTH), new),
                  pl.BlockSpec((DEC_SEQ, SB_WIDTH), new),
                  pl.BlockSpec((1, SB_WIDTH, LANES), per)],
        out_specs=pl.BlockSpec((DEC_SEQ, SB_WIDTH), lambda s: (s, 0)),
        out_shape=jax.ShapeDtypeStruct((ROWS_S, SB_WIDTH), BF16),
        scratch_shapes=[pltpu.VMEM((PAST_LEN + LANES + SUBLANES, LANES), F32)],
        compiler_params=_params(("arbitrary",)),
        name="attn_sample",
    )(ck, cv, k, v, qbd)


def _pool_kernel(*refs, tm, use_halo):
    if use_halo:
        pv_ref, halo_ref, w_ref, sc_ref, o_ref, s1, s2, s4, s8 = refs
    else:
        pv_ref, w_ref, sc_ref, o_ref, s1, s2, s4, s8 = refs
    i = pl.program_id(0)
    off = SUBLANES
    n = HALO + tm
    x = pv_ref[...]
    zero8 = jnp.zeros((off, POOL_WIDTH), F32)
    if use_halo:
        first = (i % (SEQ_PAD // tm)) == 0
        halo = jnp.where(first, 0.0, halo_ref[...])
    else:
        halo = jnp.zeros((HALO, POOL_WIDTH), F32)
    s1[0:off, :] = zero8
    s1[off:off + HALO, :] = halo
    s1[off + HALO:off + n, :] = x
    g = POOL_GROUP_DIM
    t2 = s1[off:off + n, :] + s1[off - 1:off - 1 + n, :]
    s2[0:off, :] = zero8[:, g:]
    s2[off:off + n, :] = t2[:, g:]
    t4 = t2[:, g:] + s2[off - 2:off - 2 + n, :]
    s4[0:off, :] = zero8[:, 2 * g:]
    s4[off:off + n, :] = t4[:, g:]
    t8 = t4[:, g:] + s4[off - 4:off - 4 + n, :]
    s8[0:off, :] = zero8[:, 3 * g:]
    s8[off:off + n, :] = t8[:, g:]
    t16 = t8[:, g:] + s8[off - 8:off - 8 + n, :]
    sums = (t2[HALO:, 0:g], t4[HALO:, 0:g], t8[HALO:, 0:g], t16[HALO:, :])
    if use_halo:
        idx = (i % (SEQ_PAD // tm)) * tm - FRONT + lax.broadcasted_iota(I32, (tm, g), 0)
    for gi, win in enumerate(POOL_WINDOWS):
        if use_halo:
            cnt = jnp.clip(idx + 1, 1, win).astype(F32)
        else:
            cnt = jnp.full((tm, g), float(win), F32)
        d = sums[gi] / cnt - x[:, gi * g:(gi + 1) * g]
        y = jnp.dot(d.astype(BF16), w_ref[gi], preferred_element_type=F32)
        o_ref[:, gi * g:(gi + 1) * g] = (y * sc_ref[:, gi * g:(gi + 1) * g]).astype(BF16)


def _pool(pv, w, scale, *, rows, tm, use_halo):
    g = POOL_GROUP_DIM
    in_specs = [pl.BlockSpec((tm, POOL_WIDTH), lambda i: (i, 0))]
    args = [pv]
    if use_halo:
        in_specs.append(pl.BlockSpec((HALO, POOL_WIDTH),
                                     lambda i: (jnp.maximum(i * (tm // HALO) - 1, 0), 0)))
        args.append(pv)
    in_specs += [pl.BlockSpec((4, g, g), lambda i: (0, 0, 0)),
                 pl.BlockSpec((1, POOL_WIDTH), lambda i: (0, 0))]
    args += [w, scale]
    n = SUBLANES + HALO + tm
    return pl.pallas_call(
        functools.partial(_pool_kernel, tm=tm, use_halo=use_halo),
        grid=(rows // tm,),
        in_specs=in_specs,
        out_specs=pl.BlockSpec((tm, POOL_WIDTH), lambda i: (i, 0)),
        out_shape=jax.ShapeDtypeStruct((rows, POOL_WIDTH), BF16),
        scratch_shapes=[pltpu.VMEM((n, 4 * g), F32), pltpu.VMEM((n, 3 * g), F32),
                        pltpu.VMEM((n, 2 * g), F32), pltpu.VMEM((n, g), F32)],
        compiler_params=_params(("arbitrary",)),
        name="pool_prompt" if use_halo else "pool_sample",
    )(*args)


def _mix_kernel(*refs, route):
    if route:
        (h_ref, pp_ref, ps_ref, sp_ref, ss_ref, w_ref, g_ref, r_ref,
         hmid_ref, hn_ref, rt_ref) = refs
    else:
        h_ref, pp_ref, ps_ref, sp_ref, ss_ref, w_ref, g_ref, hmid_ref, hn_ref = refs
    prompt = pl.program_id(0) < ROWS_P // TM
    pool = jnp.where(prompt, pp_ref[...], ps_ref[...])
    sb = jnp.where(prompt, sp_ref[...], ss_ref[...])
    hmid = (h_ref[...]
            + jnp.dot(pool, w_ref[0:POOL_WIDTH, :], preferred_element_type=F32)
            + jnp.dot(sb, w_ref[POOL_WIDTH:, :], preferred_element_type=F32))
    hmid_ref[...] = hmid
    hn = _rms(hmid, g_ref[...])
    hn_ref[...] = hn.astype(hn_ref.dtype)
    if route:
        tm = hn.shape[0]
        logits = jnp.dot(hn, r_ref[...], preferred_element_type=F32,
                         precision=lax.Precision.HIGHEST)
        lane = lax.broadcasted_iota(I32, (tm, LANES), 1)
        neg = jnp.float32(-jnp.inf)
        logits = jnp.where(lane < N_EXPERTS, logits, neg)
        m1 = jnp.max(logits, axis=-1, keepdims=True)
        i1 = jnp.min(jnp.where(logits == m1, lane, LANES), axis=-1, keepdims=True)
        rest = jnp.where(lane == i1, neg, logits)
        m2 = jnp.max(rest, axis=-1, keepdims=True)
        i2 = jnp.min(jnp.where(rest == m2, lane, LANES), axis=-1, keepdims=True)
        t = jnp.exp(m2 - m1)
        g1 = 1.0 / (1.0 + t)
        g2 = t / (1.0 + t)
        rt_ref[...] = (jnp.where(lane == 0, i1.astype(F32), 0.0)
                       + jnp.where(lane == 1, i2.astype(F32), 0.0)
                       + jnp.where(lane == 2, g1, 0.0) + jnp.where(lane == 3, g2, 0.0))


def _mix(h, pool_p, pool_s, sb_p, sb_s, w, gain, router=None):
    row = lambda i: (i, 0)
    fixed = lambda i: (0, 0)
    prow = lambda i: (jnp.minimum(i, ROWS_P // TM - 1), 0)
    route = router is not None
    in_specs = [pl.BlockSpec((TM, D_MODEL), row),
                pl.BlockSpec((TM, POOL_WIDTH), prow),
                pl.BlockSpec((TM, POOL_WIDTH), fixed),
                pl.BlockSpec((TM, SB_WIDTH), prow),
                pl.BlockSpec((TM, SB_WIDTH), fixed),
                pl.BlockSpec((D_MODEL, D_MODEL), fixed),
                pl.BlockSpec((1, D_MODEL), fixed)]
    out_specs = [pl.BlockSpec((TM, D_MODEL), row), pl.BlockSpec((TM, D_MODEL), row)]
    out_shape = [jax.ShapeDtypeStruct((ROWS, D_MODEL), F32),
                 jax.ShapeDtypeStruct((ROWS, D_MODEL), F32 if route else BF16)]
    args = [h, pool_p, pool_s, sb_p, sb_s, w, gain]
    if route:
        in_specs.append(pl.BlockSpec((D_MODEL, LANES), fixed))
        out_specs.append(pl.BlockSpec((TM, LANES), row))
        out_shape.append(jax.ShapeDtypeStruct((ROWS, LANES), F32))
        args.append(router)
    return pl.pallas_call(
        functools.partial(_mix_kernel, route=route),
        grid=(ROWS // TM,),
        in_specs=in_specs, out_specs=out_specs, out_shape=out_shape,
        compiler_params=_params(("arbitrary",)),
        name="mix_route" if route else "mix",
    )(*args)


def _silu_mul(g, u):
    return g * (1.0 / (1.0 + jnp.exp(-g))) * u


FF_CHUNK = 256


def _ffn_kernel(hmid_ref, hn_ref, wg_ref, wu_ref, wd_ref, o_ref, act):
    hn = hn_ref[...]
    for c in range(D_FF // FF_CHUNK):
        sl = slice(c * FF_CHUNK, (c + 1) * FF_CHUNK)
        g = jnp.dot(hn, wg_ref[:, sl], preferred_element_type=F32)
        u = jnp.dot(hn, wu_ref[:, sl], preferred_element_type=F32)
        act[:, sl] = _silu_mul(g, u).astype(BF16)
    o_ref[...] = hmid_ref[...] + jnp.dot(act[...], wd_ref[...], preferred_element_type=F32)


def _ffn(hmid, hn, wg, wu, wd):
    row = lambda i: (i, 0)
    fixed = lambda i: (0, 0)
    once = pl.Buffered(1)
    return pl.pallas_call(
        _ffn_kernel,
        grid=(ROWS // TM,),
        in_specs=[pl.BlockSpec((TM, D_MODEL), row),
                  pl.BlockSpec((TM, D_MODEL), row),
                  pl.BlockSpec((D_MODEL, D_FF), fixed, pipeline_mode=once),
                  pl.BlockSpec((D_MODEL, D_FF), fixed, pipeline_mode=once),
                  pl.BlockSpec((D_FF, D_MODEL), fixed, pipeline_mode=once)],
        out_specs=pl.BlockSpec((TM, D_MODEL), row),
        out_shape=jax.ShapeDtypeStruct((ROWS, D_MODEL), F32),
        scratch_shapes=[pltpu.VMEM((TM, D_FF), BF16)],
        compiler_params=_params(("arbitrary",)),
        name="ffn",
    )(hmid, hn, wg, wu, wd)


def _route_plan(route):
    e_flat = jnp.concatenate([route[:, 0], route[:, 1]]).astype(I32)
    order = jnp.argsort(e_flat).astype(I32)
    ex = jnp.arange(N_EXPERTS, dtype=I32)
    counts = jnp.sum((e_flat[:, None] == ex[None, :]).astype(I32), axis=0)
    gstart = jnp.cumsum(counts) - counts
    padded = ((counts + TM_E - 1) // TM_E) * TM_E
    pend = jnp.cumsum(padded)
    pstart = pend - padded
    q = jnp.arange(P_MAX, dtype=I32)
    e_q = jnp.sum((q[:, None] >= pend[None, :]).astype(I32), axis=1)
    e_c = jnp.minimum(e_q, N_EXPERTS - 1)
    within = q - pstart[e_c]
    valid = (e_q < N_EXPERTS) & (within < counts[e_c])
    pair = order[jnp.clip(gstart[e_c] + within, 0, PAIRS - 1)]
    src = jnp.where(valid, jnp.where(pair >= ROWS, pair - ROWS, pair), 0)
    spare = PAIRS + ((q // TM_E) % 2) * TM_E + (q % TM_E)
    dst = jnp.where(valid, pair, spare)
    tile_expert = e_c[::TM_E]
    n_used = (pend[-1] // TM_E).reshape(1)
    return (src.reshape(NT_E, 1, TM_E), dst.reshape(NT_E, 1, TM_E), tile_expert, n_used)


def _rowcopy_kernel(sidx_ref, didx_ref, src_ref, dst_ref, sems, *, n_steps, rows_per_step):
    s = pl.program_id(0)
    slot = s % 2

    def issue(r, _):
        pltpu.make_async_copy(src_ref.at[pl.ds(sidx_ref[0, 0, r], 1)],
                              dst_ref.at[pl.ds(didx_ref[0, 0, r], 1)],
                              sems.at[slot]).start()
        return 0

    lax.fori_loop(0, rows_per_step, issue, 0, unroll=8)

    def drain(which):
        pltpu.make_async_copy(src_ref.at[pl.ds(0, rows_per_step)],
                              dst_ref.at[pl.ds(0, rows_per_step)], sems.at[which]).wait()

    @pl.when(s > 0)
    def _():
        drain(1 - slot)

    @pl.when(s == n_steps - 1)
    def _():
        drain(slot)


def _rowcopy(sidx, didx, src, out_rows, name):
    n_steps, _, per = sidx.shape
    idx_spec = pl.BlockSpec((1, 1, per), lambda s: (s, 0, 0), memory_space=pltpu.SMEM)
    return pl.pallas_call(
        functools.partial(_rowcopy_kernel, n_steps=n_steps, rows_per_step=per),
        grid=(n_steps,),
        in_specs=[idx_spec, idx_spec, pl.BlockSpec(memory_space=pl.ANY)],
        out_specs=pl.BlockSpec(memory_space=pl.ANY),
        out_shape=jax.ShapeDtypeStruct((out_rows, src.shape[1]), src.dtype),
        scratch_shapes=[pltpu.SemaphoreType.DMA((2,))],
        compiler_params=_params(("arbitrary",)),
        name=name,
    )(sidx, didx, src)


def _experts_kernel(te_ref, nu_ref, x_ref, wg_ref, wu_ref, wd_ref, y_ref, xb, acc):
    t = pl.program_id(0)
    c = pl.program_id(1)
    last = pl.num_programs(1) - 1
    used = t < nu_ref[0]

    @pl.when(used & (c == 0))
    def _():
        xb[...] = x_ref[...].astype(BF16)

    @pl.when(used)
    def _():
        x = xb[...]
        g = jnp.dot(x, wg_ref[0], preferred_element_type=F32)
        u = jnp.dot(x, wu_ref[0], preferred_element_type=F32)
        y = jnp.dot(_silu_mul(g, u).astype(BF16), wd_ref[0], preferred_element_type=F32)

        @pl.when(c == 0)
        def _():
            acc[...] = y

        @pl.when(c > 0)
        def _():
            acc[...] += y

    @pl.when(c == last)
    def _():
        y_ref[...] = jnp.where(used, acc[...], 0.0)


def _experts(tile_expert, n_used, xs, wg, wu, wd):
    n_chunks = D_EXPERT // EX_CHUNK
    grid_spec = pltpu.PrefetchScalarGridSpec(
        num_scalar_prefetch=2,
        grid=(NT_E, n_chunks),
        in_specs=[pl.BlockSpec((TM_E, D_MODEL), lambda t, c, te, nu: (t, 0)),
                  pl.BlockSpec((1, D_MODEL, EX_CHUNK), lambda t, c, te, nu: (te[t], 0, c)),
                  pl.BlockSpec((1, D_MODEL, EX_CHUNK), lambda t, c, te, nu: (te[t], 0, c)),
                  pl.BlockSpec((1, EX_CHUNK, D_MODEL), lambda t, c, te, nu: (te[t], c, 0))],
        out_specs=pl.BlockSpec((TM_E, D_MODEL), lambda t, c, te, nu: (t, 0)),
        scratch_shapes=[pltpu.VMEM((TM_E, D_MODEL), BF16), pltpu.VMEM((TM_E, D_MODEL), F32)])
    return pl.pallas_call(
        _experts_kernel,
        grid_spec=grid_spec,
        out_shape=jax.ShapeDtypeStruct((P_MAX, D_MODEL), F32),
        compiler_params=_params(("arbitrary", "arbitrary")),
        name="experts",
    )(tile_expert, n_used, xs, wg, wu, wd)


def _combine_kernel(hmid_ref, a_ref, b_ref, rt_ref, fg_ref, yp_ref, ys_ref):
    i = pl.program_id(0)
    lane = lax.broadcasted_iota(I32, rt_ref.shape, 1)
    rt = rt_ref[...]
    g1 = jnp.sum(jnp.where(lane == 2, rt, 0.0), axis=-1, keepdims=True)
    g2 = jnp.sum(jnp.where(lane == 3, rt, 0.0), axis=-1, keepdims=True)
    y = _rms(hmid_ref[...] + g1 * a_ref[...] + g2 * b_ref[...], fg_ref[...])
    prompt = i < ROWS_P // TILE

    @pl.when(prompt & (i % SEQ_TILES != 0))
    def _():
        yp_ref[0] = y

    @pl.when(jnp.logical_not(prompt))
    def _():
        ys_ref[...] = y


def _combine(hmid, slots, route, fgain):
    n_p = ROWS_P // TILE

    def yp_map(i):
        ip = jnp.minimum(i, n_p - 1)
        return (ip // SEQ_TILES, jnp.maximum(ip % SEQ_TILES - 1, 0), 0)

    return pl.pallas_call(
        _combine_kernel,
        grid=(ROWS // TILE,),
        in_specs=[pl.BlockSpec((TILE, D_MODEL), lambda i: (i, 0)),
                  pl.BlockSpec((TILE, D_MODEL), lambda i: (i, 0)),
                  pl.BlockSpec((TILE, D_MODEL), lambda i: (i + ROWS // TILE, 0)),
                  pl.BlockSpec((TILE, LANES), lambda i: (i, 0)),
                  pl.BlockSpec((1, D_MODEL), lambda i: (0, 0))],
        out_specs=[pl.BlockSpec((1, TILE, D_MODEL), yp_map),
                   pl.BlockSpec((TILE, D_MODEL), lambda i: (jnp.maximum(i - n_p, 0), 0))],
        out_shape=[jax.ShapeDtypeStruct((BATCH, SEQ, D_MODEL), F32),
                   jax.ShapeDtypeStruct((ROWS_S, D_MODEL), F32)],
        compiler_params=_params(("arbitrary",)),
        name="combine",
    )(hmid, slots, slots, route, fgain)


def _moe(hmid, hn, route, wg, wu, wd, fgain):
    src, dst, tile_expert, n_used = _route_plan(route)
    here = jnp.arange(P_MAX, dtype=I32).reshape(NT_E, 1, TM_E)
    xs = _rowcopy(src, here, hn, P_MAX, "moe_gather")
    ys = _experts(tile_expert, n_used, xs, wg, wu, wd)
    slots = _rowcopy(here, dst, ys, SLOT_ROWS, "moe_scatter")
    return _combine(hmid, slots, route, fgain)


def _sample_qbd(q):
    qs = q[ROWS_P:].reshape(DEC_BATCH, DEC_SEQ, SB_HEADS, SB_HEAD_DIM)
    eye = jnp.eye(SB_HEADS, dtype=q.dtype)
    qbd = jnp.einsum('sthd,hg->shdgt', qs, eye)
    return qbd.reshape(DEC_BATCH, SB_WIDTH, SB_HEADS * DEC_SEQ)


def kernel(x_prompt, x_sample, cache_k, cache_v, state_pool, meta_tokens, norm_mix, w_in, pool_w, pool_scale, w_out, norm_ffn, ffn_w_gate, ffn_w_up, ffn_w_down, moe_router, moe_w_gate, moe_w_up, moe_w_down, final_norm):
    front = jnp.zeros((BATCH, FRONT, D_MODEL), F32)
    meta = jnp.broadcast_to(meta_tokens[None], (BATCH, N_META, D_MODEL))
    h = jnp.concatenate([
        jnp.concatenate([front, meta, x_prompt], axis=1).reshape(ROWS_P, D_MODEL),
        x_sample.reshape(ROWS_S, D_MODEL)], axis=0)
    perm = _key_permutation()

    outs = {name: [] for name in ("pk", "pv", "pp", "sk", "sv", "sp")}
    y_prompt = y_sample = None
    for l in range(DEPTH):
        pv, q, k, v, kp, qt3, vt3 = _inproj(h, norm_mix[l][None], w_in[l].astype(BF16), perm)

        sb_p = _attn_prompt(kp, qt3, vt3)
        sb_s = _attn_sample(cache_k[l].reshape(DEC_BATCH, PAST_LEN, SB_WIDTH),
                            cache_v[l].reshape(DEC_BATCH, PAST_LEN, SB_WIDTH),
                            k, v, _sample_qbd(q))

        pw = pool_w[l].astype(BF16)
        ps = pool_scale[l][None]
        pool_p = _pool(pv, pw, ps, rows=ROWS_P, tm=TILE, use_halo=True)
        pv_s = pv[ROWS_P:].reshape(DEC_BATCH, DEC_SEQ, POOL_WIDTH)
        pbuf = jnp.concatenate([jnp.zeros((DEC_BATCH, 1, POOL_WIDTH), F32), state_pool[l], pv_s],
                               axis=1).reshape(DEC_BATCH * 2 * DEC_SEQ, POOL_WIDTH)
        pool_s = _pool(pbuf, pw, ps, rows=DEC_BATCH * 2 * DEC_SEQ, tm=TM, use_halo=False)
        pool_s = pool_s.reshape(DEC_BATCH, 2 * DEC_SEQ, POOL_WIDTH)[:, DEC_SEQ:].reshape(ROWS_S, POOL_WIDTH)

        wo = w_out[l].astype(BF16)
        j = l // 2
        if l % 2 == 0:
            hmid, hn = _mix(h, pool_p, pool_s, sb_p, sb_s, wo, norm_ffn[l][None])
            h = _ffn(hmid, hn, ffn_w_gate[j].astype(BF16), ffn_w_up[j].astype(BF16),
                     ffn_w_down[j].astype(BF16))
        else:
            router = jnp.pad(moe_router[j], ((0, 0), (0, LANES - N_EXPERTS)))
            hmid, hn, route = _mix(h, pool_p, pool_s, sb_p, sb_s, wo, norm_ffn[l][None], router)
            y_prompt, y_sample = _moe(hmid, hn, route, moe_w_gate[j].astype(BF16),
                                      moe_w_up[j].astype(BF16), moe_w_down[j].astype(BF16),
                                      final_norm[None])

        kp4 = k[:ROWS_P].reshape(BATCH, SEQ_PAD, SB_HEADS, SB_HEAD_DIM)[:, FRONT:]
        vp4 = v[:ROWS_P].reshape(BATCH, SEQ_PAD, SB_HEADS, SB_HEAD_DIM)[:, FRONT:]
        outs["pk"].append(kp4)
        outs["pv"].append(vp4)
        outs["pp"].append(pv[:ROWS_P].reshape(BATCH, SEQ_PAD, POOL_WIDTH)[:, -POOL_STATE:])
        outs["sk"].append(k[ROWS_P:].reshape(DEC_BATCH, DEC_SEQ, SB_HEADS, SB_HEAD_DIM))
        outs["sv"].append(v[ROWS_P:].reshape(DEC_BATCH, DEC_SEQ, SB_HEADS, SB_HEAD_DIM))
        outs["sp"].append(pv_s[:, -POOL_STATE:])

    y_sample = y_sample.reshape(DEC_BATCH, DEC_SEQ, D_MODEL)
    return (y_prompt, y_sample, jnp.stack(outs["pk"]), jnp.stack(outs["pv"]), jnp.stack(outs["pp"]),
            jnp.stack(outs["sk"]), jnp.stack(outs["sv"]), jnp.stack(outs["sp"]))
```

```python
import functools

import jax
import jax.numpy as jnp
from jax import lax
from jax.experimental import pallas as pl
from jax.experimental.pallas import tpu as pltpu

F32 = jnp.float32
BF16 = jnp.bfloat16
I32 = jnp.int32

D_MODEL = 1024
BATCH = 2
SEQ = 8192
DEPTH = 2
DEC_BATCH = 32
DEC_SEQ = 16
PAST_LEN = 1024
N_META = 16
POOL_WIDTH = 512
POOL_WINDOWS = (2, 4, 8, 16)
POOL_GROUP_DIM = 128
POOL_STATE = 15
SB_HEADS = 8
SB_HEAD_DIM = 64
SB_WIDTH = 512
IN_WIDTH = 2048
D_FF = 2816
N_EXPERTS = 8
D_EXPERT = 3584
EPS = 1e-6

LANES = 128
SUBLANES = 8
TILE = 256
SEG = TILE // SUBLANES
FRONT = TILE - N_META
SEQ_PAD = FRONT + N_META + SEQ
SEQ_TILES = SEQ_PAD // TILE
ROWS_P = BATCH * SEQ_PAD
ROWS_S = DEC_BATCH * DEC_SEQ
ROWS = ROWS_P + ROWS_S
TM = 512
HALO = 16
VMEM_LIMIT = 56 * 1024 * 1024

TM_E = 512
PAIRS = 2 * ROWS
P_MAX = PAIRS + N_EXPERTS * TM_E
NT_E = P_MAX // TM_E
SLOT_ROWS = PAIRS + 2 * TM_E
EX_CHUNK = 512


def _params(sem, vmem=VMEM_LIMIT):
    return pltpu.CompilerParams(dimension_semantics=sem, vmem_limit_bytes=vmem)


def _rms(x, gain):
    ms = jnp.mean(x * x, axis=-1, keepdims=True)
    return x * lax.rsqrt(ms + EPS) * gain


def _inproj_kernel(*refs):
    x_ref, g_ref, w_ref, perm_ref = refs[:4]
    pv_ref, q_ref, k_ref, v_ref, kp_ref, qt_ref, vt_ref = refs[-7:]
    hn = _rms(x_ref[...], g_ref[...]).astype(BF16)
    pv_ref[...] = jnp.dot(hn, w_ref[:, 0:512], preferred_element_type=F32)
    q = jnp.dot(hn, w_ref[:, 512:1024], preferred_element_type=F32) * (SB_HEAD_DIM ** -0.5)
    k = jnp.dot(hn, w_ref[:, 1024:1536], preferred_element_type=F32)
    v = jnp.dot(hn, w_ref[:, 1536:2048], preferred_element_type=F32)
    q_ref[...] = q.astype(BF16)
    k_ref[...] = k
    v_ref[...] = v
    perm = perm_ref[...]
    for half in range(TM // TILE):
        sl = slice(half * TILE, (half + 1) * TILE)
        kp_ref[sl, :] = jnp.dot(perm, k[sl].astype(BF16), preferred_element_type=F32).astype(BF16)
        vp = jnp.dot(perm, v[sl].astype(BF16), preferred_element_type=F32)
        vt_ref[half] = vp.T.astype(BF16)
        qt_ref[half] = (q[sl] * 0.5).T.astype(BF16)


def _inproj(h, gain, w, perm, layer, kv_bufs):
    row = lambda i: (i, 0)
    fixed = lambda i: (0, 0)
    tiles = lambda i: (i, 0, 0)
    lrow = lambda i: (layer, i, 0)
    per = TM // TILE
    in_specs = [pl.BlockSpec((TM, D_MODEL), row),
                pl.BlockSpec((1, D_MODEL), fixed),
                pl.BlockSpec((D_MODEL, IN_WIDTH), fixed),
                pl.BlockSpec((TILE, TILE), fixed),
                pl.BlockSpec(memory_space=pl.ANY), pl.BlockSpec(memory_space=pl.ANY)]
    args = [h, gain, w, perm, *kv_bufs]
    aliases = {4: 2, 5: 3}
    return pl.pallas_call(
        _inproj_kernel,
        grid=(ROWS // TM,),
        in_specs=in_specs,
        out_specs=[pl.BlockSpec((TM, 512), row), pl.BlockSpec((TM, 512), row),
                   pl.BlockSpec((None, TM, 512), lrow), pl.BlockSpec((None, TM, 512), lrow),
                   pl.BlockSpec((TM, 512), row)]
        + [pl.BlockSpec((per, SB_WIDTH, TILE), tiles)] * 2,
        out_shape=[jax.ShapeDtypeStruct((ROWS, 512), F32),
                   jax.ShapeDtypeStruct((ROWS, 512), BF16),
                   jax.ShapeDtypeStruct((DEPTH, ROWS, 512), F32),
                   jax.ShapeDtypeStruct((DEPTH, ROWS, 512), F32),
                   jax.ShapeDtypeStruct((ROWS, 512), BF16),
                   jax.ShapeDtypeStruct((ROWS // TILE, SB_WIDTH, TILE), BF16),
                   jax.ShapeDtypeStruct((ROWS // TILE, SB_WIDTH, TILE), BF16)],
        input_output_aliases=aliases,
        compiler_params=_params(("arbitrary",)),
        name="inproj",
    )(*args)


def _key_permutation():
    r = jnp.arange(TILE)
    src = (r % SUBLANES) * SEG + r // SUBLANES
    return (src[:, None] == jnp.arange(TILE)[None, :]).astype(BF16)


def _sb_half(zh, carry, lane0, masked):
    om = 0.5 - 0.5 * jnp.tanh(zh)
    if masked:
        r = lax.broadcasted_iota(I32, (TILE, LANES), 0)
        c = lax.broadcasted_iota(I32, (TILE, LANES), 1) + lane0
        key = ((r & (SUBLANES - 1)) * SEG) + (r >> 3)
        om = jnp.where(key < c, om, 1.0)
    run = jnp.ones((SUBLANES, LANES), F32)
    parts = [None] * SEG
    for a in reversed(range(SEG)):
        nxt = run * om[a * SUBLANES:(a + 1) * SUBLANES]
        parts[a] = run - nxt
        run = nxt
    sub = lax.broadcasted_iota(I32, (SUBLANES, LANES), 0)
    y = run
    for k in (1, 2, 4):
        y = y * jnp.where(sub + k < SUBLANES, pltpu.roll(y, SUBLANES - k, 0), 1.0)
    off = carry * jnp.where(sub + 1 < SUBLANES, pltpu.roll(y, SUBLANES - 1, 0), 1.0)
    new_carry = carry * jnp.broadcast_to(y[0:1, :], (SUBLANES, LANES))
    a_t = jnp.concatenate([p * off for p in parts], axis=0).astype(BF16)
    return a_t, new_carry


def _sb_tile(z, carry, masked):
    outs = []
    carries = []
    for lh in range(TILE // LANES):
        ls = slice(lh * LANES, (lh + 1) * LANES)
        a, c = _sb_half(z[:, ls], carry[:, ls], lh * LANES, masked)
        outs.append(a)
        carries.append(c)
    return jnp.concatenate(outs, axis=1), jnp.concatenate(carries, axis=1)


def _attn_prompt_kernel(nt_ref, k_ref, qt_ref, vt_ref, o_ref, zbuf, abuf):
    half = SB_HEAD_DIM
    row = lax.broadcasted_iota(I32, (LANES, TILE), 0)

    def k_tile(j):
        return k_ref[pl.ds(pl.multiple_of(j * TILE, TILE), TILE), :]

    def q_tile(i, _):
        qt = qt_ref[i].astype(F32)
        qh = (jnp.where(row < half, qt, 0.0).astype(BF16),
              jnp.where(row >= half, qt, 0.0).astype(BF16))
        ones = jnp.ones((SUBLANES, TILE), F32)

        kd = k_tile(i)
        carries = []
        for h in range(2):
            z = jnp.dot(kd, qh[h], preferred_element_type=F32)
            a, c = _sb_tile(z, ones, True)
            abuf[0, h] = a
            abuf[1, h] = jnp.zeros((TILE, TILE), BF16)
            carries.append(c)
        k0 = k_tile(jnp.maximum(i - 1, 0))
        k1 = k_tile(jnp.maximum(i - 2, 0))
        for h in range(2):
            zbuf[0, h] = jnp.dot(k0, qh[h], preferred_element_type=F32)
            zbuf[1, h] = jnp.dot(k1, qh[h], preferred_element_type=F32)

        def weighted_values(acc, v0, v1):
            out = []
            for h in range(2):
                rows = slice(h * half, (h + 1) * half)
                out.append(acc[h]
                           + jnp.dot(v0[rows], abuf[0, h], preferred_element_type=F32)
                           + jnp.dot(v1[rows], abuf[1, h], preferred_element_type=F32))
            return out

        def trip(t, st):
            c0, c1, acc0, acc1, p0, p1 = st
            j0 = i - 1 - 2 * t
            j1 = jnp.maximum(j0 - 1, 0)
            acc = weighted_values((acc0, acc1), vt_ref[p0], vt_ref[p1])
            cs = [c0, c1]
            for h in range(2):
                for s in range(2):
                    a, cs[h] = _sb_tile(zbuf[s, h], cs[h], False)
                    abuf[s, h] = a
            n0 = k_tile(jnp.maximum(j0 - 2, 0))
            n1 = k_tile(jnp.maximum(j0 - 3, 0))
            for h in range(2):
                zbuf[0, h] = jnp.dot(n0, qh[h], preferred_element_type=F32)
                zbuf[1, h] = jnp.dot(n1, qh[h], preferred_element_type=F32)
            return cs[0], cs[1], acc[0], acc[1], j0, j1

        zeros = jnp.zeros((half, TILE), F32)
        trips = (i + 1) // 2
        st = lax.fori_loop(0, trips, trip, (carries[0], carries[1], zeros, zeros, i, i))
        keep1 = (i == 0) | ((i & 1) == 0)
        v1 = jnp.where(keep1, vt_ref[st[5]].astype(F32), 0.0).astype(BF16)
        acc = weighted_values((st[2], st[3]), vt_ref[st[4]], v1)
        out_t = jnp.concatenate(acc, axis=0)
        o_ref[pl.ds(pl.multiple_of(i * TILE, TILE), TILE), :] = out_t.T.astype(BF16)
        return 0

    lax.fori_loop(0, nt_ref[0], q_tile, 0)


def _attn_prompt(kp, qt3, vt3, *, batch=BATCH, n_tiles=SEQ_TILES):
    rows = n_tiles * TILE
    return pl.pallas_call(
        _attn_prompt_kernel,
        grid=(batch, SB_HEADS // 2),
        in_specs=[pl.BlockSpec(memory_space=pltpu.SMEM),
                  pl.BlockSpec((rows, LANES), lambda p, g: (p, g)),
                  pl.BlockSpec((n_tiles, LANES, TILE), lambda p, g: (p, g, 0)),
                  pl.BlockSpec((n_tiles, LANES, TILE), lambda p, g: (p, g, 0))],
        out_specs=pl.BlockSpec((rows, LANES), lambda p, g: (p, g)),
        out_shape=jax.ShapeDtypeStruct((batch * rows, SB_WIDTH), BF16),
        scratch_shapes=[pltpu.VMEM((2, 2, TILE, TILE), F32),
                        pltpu.VMEM((2, 2, TILE, TILE), BF16)],
        compiler_params=_params(("arbitrary", "arbitrary")),
        name="attn_prompt",
    )(jnp.full((1,), n_tiles, I32), kp, qt3, vt3)


def _rev_excl_cumprod(om, scr):
    n = om.shape[0]
    sub = lax.broadcasted_iota(I32, (n, LANES), 0) & (SUBLANES - 1)
    scr[n:n + SUBLANES, :] = jnp.ones((SUBLANES, LANES), F32)
    y = om
    for k in (1, 2, 4):
        scr[0:n, :] = y
        y = y * jnp.where(sub + k < SUBLANES, scr[k:n + k, :], 1.0)
    scr[0:n, :] = y
    ex = jnp.where(sub + 1 < SUBLANES, scr[1:n + 1, :], 1.0)
    run = jnp.ones((SUBLANES, LANES), F32)
    out = [None] * (n // SUBLANES)
    for a in reversed(range(n // SUBLANES)):
        sl = slice(a * SUBLANES, (a + 1) * SUBLANES)
        out[a] = ex[sl] * run
        run = run * jnp.broadcast_to(y[a * SUBLANES:a * SUBLANES + 1, :], (SUBLANES, LANES))
    return jnp.concatenate(out, axis=0)


def _attn_sample_kernel(ck_ref, cv_ref, kn_ref, vn_ref, qbd_ref, o_ref, scr):
    qbd = qbd_ref[0]
    pad = jnp.zeros((LANES - DEC_SEQ, SB_WIDTH), F32)
    kn = jnp.concatenate([kn_ref[...], pad], axis=0).astype(BF16)
    vn = jnp.concatenate([vn_ref[...], pad], axis=0).astype(BF16)
    zc = jnp.dot(ck_ref[0].astype(BF16), qbd, preferred_element_type=F32)
    zn = jnp.dot(kn, qbd, preferred_element_type=F32)
    z = jnp.concatenate([zc, zn], axis=0)
    n = PAST_LEN + LANES
    r = lax.broadcasted_iota(I32, (n, LANES), 0)
    c = lax.broadcasted_iota(I32, (n, LANES), 1)
    valid = (r < PAST_LEN) | ((r - PAST_LEN) < (c & (DEC_SEQ - 1)))
    e = jnp.exp(z)
    om = 1.0 / (1.0 + e)
    beta = jnp.where(valid, 1.0 - om, 0.0)
    om = jnp.where(valid, om, 1.0)
    a_t = beta * _rev_excl_cumprod(om, scr)
    ac = a_t[0:PAST_LEN].T.astype(BF16)
    an = a_t[PAST_LEN:n].T.astype(BF16)
    p = jnp.dot(ac, cv_ref[0].astype(BF16), preferred_element_type=F32)
    p = p + jnp.dot(an, vn, preferred_element_type=F32)
    pr = lax.broadcasted_iota(I32, (LANES, SB_WIDTH), 0)
    pc = lax.broadcasted_iota(I32, (LANES, SB_WIDTH), 1)
    p = jnp.where((pr >> 4) == (pc >> 6), p, 0.0)
    out = p[0:DEC_SEQ]
    for h in range(1, SB_HEADS):
        out = out + p[h * DEC_SEQ:(h + 1) * DEC_SEQ]
    o_ref[...] = out.astype(BF16)


def _attn_sample(ck, cv, k, v, qbd, layer):
    new = lambda s: (layer, ROWS_P // DEC_SEQ + s, 0)
    per = lambda s: (s, 0, 0)
    return pl.pallas_call(
        _attn_sample_kernel,
        grid=(DEC_BATCH,),
        in_specs=[pl.BlockSpec((1, PAST_LEN, SB_WIDTH), per),
                  pl.BlockSpec((1, PAST_LEN, SB_WIDTH), per),
                  pl.BlockSpec((None, DEC_SEQ, SB_WIDTH), new),
                  pl.BlockSpec((None, DEC_SEQ, SB_WIDTH), new),
                  pl.BlockSpec((1, SB_WIDTH, LANES), per)],
        out_specs=pl.BlockSpec((DEC_SEQ, SB_WIDTH), lambda s: (s, 0)),
        out_shape=jax.ShapeDtypeStruct((ROWS_S, SB_WIDTH), BF16),
        scratch_shapes=[pltpu.VMEM((PAST_LEN + LANES + SUBLANES, LANES), F32)],
        compiler_params=_params(("arbitrary",)),
        name="attn_sample",
    )(ck, cv, k, v, qbd)


def _pool_kernel(*refs, tm, use_halo):
    if use_halo:
        pv_ref, halo_ref, w_ref, sc_ref, o_ref, s1, s2, s4, s8 = refs
    else:
        pv_ref, w_ref, sc_ref, o_ref, s1, s2, s4, s8 = refs
    i = pl.program_id(0)
    off = SUBLANES
    n = HALO + tm
    x = pv_ref[...]
    zero8 = jnp.zeros((off, POOL_WIDTH), F32)
    if use_halo:
        first = (i % (SEQ_PAD // tm)) == 0
        halo = jnp.where(first, 0.0, halo_ref[...])
    else:
        halo = jnp.zeros((HALO, POOL_WIDTH), F32)
    s1[0:off, :] = zero8
    s1[off:off + HALO, :] = halo
    s1[off + HALO:off + n, :] = x
    g = POOL_GROUP_DIM
    t2 = s1[off:off + n, :] + s1[off - 1:off - 1 + n, :]
    s2[0:off, :] = zero8[:, g:]
    s2[off:off + n, :] = t2[:, g:]
    t4 = t2[:, g:] + s2[off - 2:off - 2 + n, :]
    s4[0:off, :] = zero8[:, 2 * g:]
    s4[off:off + n, :] = t4[:, g:]
    t8 = t4[:, g:] + s4[off - 4:off - 4 + n, :]
    s8[0:off, :] = zero8[:, 3 * g:]
    s8[off:off + n, :] = t8[:, g:]
    t16 = t8[:, g:] + s8[off - 8:off - 8 + n, :]
    sums = (t2[HALO:, 0:g], t4[HALO:, 0:g], t8[HALO:, 0:g], t16[HALO:, :])
    if use_halo:
        idx = (i % (SEQ_PAD // tm)) * tm - FRONT + lax.broadcasted_iota(I32, (tm, g), 0)
    for gi, win in enumerate(POOL_WINDOWS):
        if use_halo:
            cnt = jnp.clip(idx + 1, 1, win).astype(F32)
        else:
            cnt = jnp.full((tm, g), float(win), F32)
        d = sums[gi] / cnt - x[:, gi * g:(gi + 1) * g]
        y = jnp.dot(d.astype(BF16), w_ref[gi], preferred_element_type=F32)
        o_ref[:, gi * g:(gi + 1) * g] = (y * sc_ref[:, gi * g:(gi + 1) * g]).astype(BF16)


def _pool(pv, w, scale, *, rows, tm, use_halo):
    g = POOL_GROUP_DIM
    in_specs = [pl.BlockSpec((tm, POOL_WIDTH), lambda i: (i, 0))]
    args = [pv]
    if use_halo:
        in_specs.append(pl.BlockSpec((HALO, POOL_WIDTH),
                                     lambda i: (jnp.maximum(i * (tm // HALO) - 1, 0), 0)))
        args.append(pv)
    in_specs += [pl.BlockSpec((4, g, g), lambda i: (0, 0, 0)),
                 pl.BlockSpec((1, POOL_WIDTH), lambda i: (0, 0))]
    args += [w, scale]
    n = SUBLANES + HALO + tm
    return pl.pallas_call(
        functools.partial(_pool_kernel, tm=tm, use_halo=use_halo),
        grid=(rows // tm,),
        in_specs=in_specs,
        out_specs=pl.BlockSpec((tm, POOL_WIDTH), lambda i: (i, 0)),
        out_shape=jax.ShapeDtypeStruct((rows, POOL_WIDTH), BF16),
        scratch_shapes=[pltpu.VMEM((n, 4 * g), F32), pltpu.VMEM((n, 3 * g), F32),
                        pltpu.VMEM((n, 2 * g), F32), pltpu.VMEM((n, g), F32)],
        compiler_params=_params(("arbitrary",)),
        name="pool_prompt" if use_halo else "pool_sample",
    )(*args)


def _mix_kernel(*refs, route):
    if route:
        (h_ref, pp_ref, ps_ref, sp_ref, ss_ref, w_ref, g_ref, r_ref,
         hmid_ref, hn_ref, rt_ref) = refs
    else:
        h_ref, pp_ref, ps_ref, sp_ref, ss_ref, w_ref, g_ref, hmid_ref, hn_ref = refs
    prompt = pl.program_id(0) < ROWS_P // TM
    pool = jnp.where(prompt, pp_ref[...], ps_ref[...])
    sb = jnp.where(prompt, sp_ref[...], ss_ref[...])
    hmid = (h_ref[...]
            + jnp.dot(pool, w_ref[0:POOL_WIDTH, :], preferred_element_type=F32)
            + jnp.dot(sb, w_ref[POOL_WIDTH:, :], preferred_element_type=F32))
    hmid_ref[...] = hmid
    hn = _rms(hmid, g_ref[...])
    hn_ref[...] = hn.astype(hn_ref.dtype)
    if route:
        tm = hn.shape[0]
        h_hi = hn.astype(BF16)
        h_lo = (hn - h_hi.astype(F32)).astype(BF16)
        r = r_ref[...]
        r_hi = r.astype(BF16)
        r_lo = (r - r_hi.astype(F32)).astype(BF16)
        logits = (jnp.dot(h_hi, r_hi, preferred_element_type=F32)
                  + jnp.dot(h_hi, r_lo, preferred_element_type=F32)
                  + jnp.dot(h_lo, r_hi, preferred_element_type=F32))
        lane = lax.broadcasted_iota(I32, (tm, LANES), 1)
        neg = jnp.float32(-jnp.inf)
        logits = jnp.where(lane < N_EXPERTS, logits, neg)
        m1 = jnp.max(logits, axis=-1, keepdims=True)
        i1 = jnp.min(jnp.where(logits == m1, lane, LANES), axis=-1, keepdims=True)
        rest = jnp.where(lane == i1, neg, logits)
        m2 = jnp.max(rest, axis=-1, keepdims=True)
        i2 = jnp.min(jnp.where(rest == m2, lane, LANES), axis=-1, keepdims=True)
        t = jnp.exp(m2 - m1)
        g1 = 1.0 / (1.0 + t)
        g2 = t / (1.0 + t)
        rt_ref[...] = (jnp.where(lane == 0, i1.astype(F32), 0.0)
                       + jnp.where(lane == 1, i2.astype(F32), 0.0)
                       + jnp.where(lane == 2, g1, 0.0) + jnp.where(lane == 3, g2, 0.0))


def _mix(h, pool_p, pool_s, sb_p, sb_s, w, gain, router=None):
    row = lambda i: (i, 0)
    fixed = lambda i: (0, 0)
    prow = lambda i: (jnp.minimum(i, ROWS_P // TM - 1), 0)
    route = router is not None
    in_specs = [pl.BlockSpec((TM, D_MODEL), row),
                pl.BlockSpec((TM, POOL_WIDTH), prow),
                pl.BlockSpec((TM, POOL_WIDTH), fixed),
                pl.BlockSpec((TM, SB_WIDTH), prow),
                pl.BlockSpec((TM, SB_WIDTH), fixed),
                pl.BlockSpec((D_MODEL, D_MODEL), fixed),
                pl.BlockSpec((1, D_MODEL), fixed)]
    out_specs = [pl.BlockSpec((TM, D_MODEL), row), pl.BlockSpec((TM, D_MODEL), row)]
    out_shape = [jax.ShapeDtypeStruct((ROWS, D_MODEL), F32),
                 jax.ShapeDtypeStruct((ROWS, D_MODEL), F32 if route else BF16)]
    args = [h, pool_p, pool_s, sb_p, sb_s, w, gain]
    if route:
        in_specs.append(pl.BlockSpec((D_MODEL, LANES), fixed))
        out_specs.append(pl.BlockSpec((TM, LANES), row))
        out_shape.append(jax.ShapeDtypeStruct((ROWS, LANES), F32))
        args.append(router)
    return pl.pallas_call(
        functools.partial(_mix_kernel, route=route),
        grid=(ROWS // TM,),
        in_specs=in_specs, out_specs=out_specs, out_shape=out_shape,
        compiler_params=_params(("arbitrary",)),
        name="mix_route" if route else "mix",
    )(*args)


def _silu_mul(g, u):
    return g * (1.0 / (1.0 + jnp.exp(-g))) * u


FF_CHUNK = 256


def _ffn_kernel(hmid_ref, hn_ref, wg_ref, wu_ref, wd_ref, o_ref, act):
    hn = hn_ref[...]
    for c in range(D_FF // FF_CHUNK):
        sl = slice(c * FF_CHUNK, (c + 1) * FF_CHUNK)
        g = jnp.dot(hn, wg_ref[:, sl], preferred_element_type=F32)
        u = jnp.dot(hn, wu_ref[:, sl], preferred_element_type=F32)
        act[:, sl] = _silu_mul(g, u).astype(BF16)
    o_ref[...] = hmid_ref[...] + jnp.dot(act[...], wd_ref[...], preferred_element_type=F32)


def _ffn(hmid, hn, wg, wu, wd):
    row = lambda i: (i, 0)
    fixed = lambda i: (0, 0)
    once = pl.Buffered(1)
    return pl.pallas_call(
        _ffn_kernel,
        grid=(ROWS // TM,),
        in_specs=[pl.BlockSpec((TM, D_MODEL), row),
                  pl.BlockSpec((TM, D_MODEL), row),
                  pl.BlockSpec((D_MODEL, D_FF), fixed, pipeline_mode=once),
                  pl.BlockSpec((D_MODEL, D_FF), fixed, pipeline_mode=once),
                  pl.BlockSpec((D_FF, D_MODEL), fixed, pipeline_mode=once)],
        out_specs=pl.BlockSpec((TM, D_MODEL), row),
        out_shape=jax.ShapeDtypeStruct((ROWS, D_MODEL), F32),
        scratch_shapes=[pltpu.VMEM((TM, D_FF), BF16)],
        compiler_params=_params(("arbitrary",)),
        name="ffn",
    )(hmid, hn, wg, wu, wd)


def _route_plan(route):
    e_flat = jnp.concatenate([route[:, 0], route[:, 1]]).astype(I32)
    order = jnp.argsort(e_flat).astype(I32)
    ex = jnp.arange(N_EXPERTS, dtype=I32)
    counts = jnp.sum((e_flat[:, None] == ex[None, :]).astype(I32), axis=0)
    gstart = jnp.cumsum(counts) - counts
    padded = ((counts + TM_E - 1) // TM_E) * TM_E
    pend = jnp.cumsum(padded)
    pstart = pend - padded
    q = jnp.arange(P_MAX, dtype=I32)
    e_q = jnp.sum((q[:, None] >= pend[None, :]).astype(I32), axis=1)
    e_c = jnp.minimum(e_q, N_EXPERTS - 1)
    within = q - pstart[e_c]
    valid = (e_q < N_EXPERTS) & (within < counts[e_c])
    pair = order[jnp.clip(gstart[e_c] + within, 0, PAIRS - 1)]
    src = jnp.where(valid, jnp.where(pair >= ROWS, pair - ROWS, pair), 0)
    spare = PAIRS + ((q // TM_E) % 2) * TM_E + (q % TM_E)
    dst = jnp.where(valid, pair, spare)
    tile_expert = e_c[::TM_E]
    n_used = (pend[-1] // TM_E).reshape(1)
    return (src.reshape(NT_E, 1, TM_E), dst.reshape(NT_E, 1, TM_E), tile_expert, n_used)


def _gather_kernel(idx_ref, src_ref, out_ref, sem):
    per = out_ref.shape[0]

    def issue(r, _):
        pltpu.make_async_copy(src_ref.at[pl.ds(idx_ref[0, 0, r], 1)],
                              out_ref.at[pl.ds(r, 1)], sem).start()
        return 0

    lax.fori_loop(0, per, issue, 0, unroll=8)
    pltpu.make_async_copy(src_ref.at[pl.ds(0, per)], out_ref, sem).wait()


def _gather_rows(idx, src):
    n_steps, _, per = idx.shape
    return pl.pallas_call(
        _gather_kernel,
        grid=(n_steps,),
        in_specs=[pl.BlockSpec((1, 1, per), lambda s: (s, 0, 0), memory_space=pltpu.SMEM),
                  pl.BlockSpec(memory_space=pl.ANY)],
        out_specs=pl.BlockSpec((per, src.shape[1]), lambda s: (s, 0)),
        out_shape=jax.ShapeDtypeStruct((n_steps * per, src.shape[1]), src.dtype),
        scratch_shapes=[pltpu.SemaphoreType.DMA(())],
        compiler_params=_params(("arbitrary",)),
        name="moe_gather",
    )(idx, src)


def _scatter_kernel(idx_ref, in_ref, dst_ref, sem):
    per = in_ref.shape[0]

    def issue(r, _):
        pltpu.make_async_copy(in_ref.at[pl.ds(r, 1)],
                              dst_ref.at[pl.ds(idx_ref[0, 0, r], 1)], sem).start()
        return 0

    lax.fori_loop(0, per, issue, 0, unroll=8)
    pltpu.make_async_copy(in_ref, dst_ref.at[pl.ds(0, per)], sem).wait()


def _scatter_rows(idx, src, out_rows):
    n_steps, _, per = idx.shape
    return pl.pallas_call(
        _scatter_kernel,
        grid=(n_steps,),
        in_specs=[pl.BlockSpec((1, 1, per), lambda s: (s, 0, 0), memory_space=pltpu.SMEM),
                  pl.BlockSpec((per, src.shape[1]), lambda s: (s, 0))],
        out_specs=pl.BlockSpec(memory_space=pl.ANY),
        out_shape=jax.ShapeDtypeStruct((out_rows, src.shape[1]), src.dtype),
        scratch_shapes=[pltpu.SemaphoreType.DMA(())],
        compiler_params=_params(("arbitrary",)),
        name="moe_scatter",
    )(idx, src)


def _experts_kernel(te_ref, nu_ref, x_ref, wg_ref, wu_ref, wd_ref, y_ref, xb, acc):
    t = pl.program_id(0)
    c = pl.program_id(1)
    last = pl.num_programs(1) - 1
    used = t < nu_ref[0]

    @pl.when(used & (c == 0))
    def _():
        xb[...] = x_ref[...].astype(BF16)

    @pl.when(used)
    def _():
        x = xb[...]
        g = jnp.dot(x, wg_ref[0], preferred_element_type=F32)
        u = jnp.dot(x, wu_ref[0], preferred_element_type=F32)
        y = jnp.dot(_silu_mul(g, u).astype(BF16), wd_ref[0], preferred_element_type=F32)

        @pl.when(c == 0)
        def _():
            acc[...] = y

        @pl.when(c > 0)
        def _():
            acc[...] += y

    @pl.when(c == last)
    def _():
        y_ref[...] = jnp.where(used, acc[...], 0.0)


def _experts(tile_expert, n_used, xs, wg, wu, wd):
    n_chunks = D_EXPERT // EX_CHUNK
    grid_spec = pltpu.PrefetchScalarGridSpec(
        num_scalar_prefetch=2,
        grid=(NT_E, n_chunks),
        in_specs=[pl.BlockSpec((TM_E, D_MODEL), lambda t, c, te, nu: (t, 0)),
                  pl.BlockSpec((1, D_MODEL, EX_CHUNK), lambda t, c, te, nu: (te[t], 0, c)),
                  pl.BlockSpec((1, D_MODEL, EX_CHUNK), lambda t, c, te, nu: (te[t], 0, c)),
                  pl.BlockSpec((1, EX_CHUNK, D_MODEL), lambda t, c, te, nu: (te[t], c, 0))],
        out_specs=pl.BlockSpec((TM_E, D_MODEL), lambda t, c, te, nu: (t, 0)),
        scratch_shapes=[pltpu.VMEM((TM_E, D_MODEL), BF16), pltpu.VMEM((TM_E, D_MODEL), F32)])
    return pl.pallas_call(
        _experts_kernel,
        grid_spec=grid_spec,
        out_shape=jax.ShapeDtypeStruct((P_MAX, D_MODEL), F32),
        compiler_params=_params(("arbitrary", "arbitrary")),
        name="experts",
    )(tile_expert, n_used, xs, wg, wu, wd)


def _combine_kernel(hmid_ref, a_ref, b_ref, rt_ref, fg_ref, yp_ref, ys_ref):
    i = pl.program_id(0)
    lane = lax.broadcasted_iota(I32, rt_ref.shape, 1)
    rt = rt_ref[...]
    g1 = jnp.sum(jnp.where(lane == 2, rt, 0.0), axis=-1, keepdims=True)
    g2 = jnp.sum(jnp.where(lane == 3, rt, 0.0), axis=-1, keepdims=True)
    y = _rms(hmid_ref[...] + g1 * a_ref[...] + g2 * b_ref[...], fg_ref[...])
    prompt = i < ROWS_P // TILE

    @pl.when(prompt & (i % SEQ_TILES != 0))
    def _():
        yp_ref[0] = y

    @pl.when(jnp.logical_not(prompt))
    def _():
        ys_ref[...] = y


def _combine(hmid, slots, route, fgain):
    n_p = ROWS_P // TILE

    def yp_map(i):
        ip = jnp.minimum(i, n_p - 1)
        return (ip // SEQ_TILES, jnp.maximum(ip % SEQ_TILES - 1, 0), 0)

    return pl.pallas_call(
        _combine_kernel,
        grid=(ROWS // TILE,),
        in_specs=[pl.BlockSpec((TILE, D_MODEL), lambda i: (i, 0)),
                  pl.BlockSpec((TILE, D_MODEL), lambda i: (i, 0)),
                  pl.BlockSpec((TILE, D_MODEL), lambda i: (i + ROWS // TILE, 0)),
                  pl.BlockSpec((TILE, LANES), lambda i: (i, 0)),
                  pl.BlockSpec((1, D_MODEL), lambda i: (0, 0))],
        out_specs=[pl.BlockSpec((1, TILE, D_MODEL), yp_map),
                   pl.BlockSpec((TILE, D_MODEL), lambda i: (jnp.maximum(i - n_p, 0), 0))],
        out_shape=[jax.ShapeDtypeStruct((BATCH, SEQ, D_MODEL), F32),
                   jax.ShapeDtypeStruct((ROWS_S, D_MODEL), F32)],
        compiler_params=_params(("arbitrary",)),
        name="combine",
    )(hmid, slots, slots, route, fgain)


def _moe(hmid, hn, route, wg, wu, wd, fgain):
    src, dst, tile_expert, n_used = _route_plan(route)
    xs = _gather_rows(src, hn)
    ys = _experts(tile_expert, n_used, xs, wg, wu, wd)
    slots = _scatter_rows(dst, ys, SLOT_ROWS)
    return _combine(hmid, slots, route, fgain)


def _sample_qbd(q):
    qs = q[ROWS_P:].reshape(DEC_BATCH, DEC_SEQ, SB_HEADS, SB_HEAD_DIM)
    eye = jnp.eye(SB_HEADS, dtype=q.dtype)
    qbd = jnp.einsum('sthd,hg->shdgt', qs, eye)
    return qbd.reshape(DEC_BATCH, SB_WIDTH, SB_HEADS * DEC_SEQ)


def kernel(x_prompt, x_sample, cache_k, cache_v, state_pool, meta_tokens, norm_mix, w_in, pool_w, pool_scale, w_out, norm_ffn, ffn_w_gate, ffn_w_up, ffn_w_down, moe_router, moe_w_gate, moe_w_up, moe_w_down, final_norm):
    front = jnp.zeros((BATCH, FRONT, D_MODEL), F32)
    meta = jnp.broadcast_to(meta_tokens[None], (BATCH, N_META, D_MODEL))
    h = jnp.concatenate([
        jnp.concatenate([front, meta, x_prompt], axis=1).reshape(ROWS_P, D_MODEL),
        x_sample.reshape(ROWS_S, D_MODEL)], axis=0)
    perm = _key_permutation()

    outs = {name: [] for name in ("pp", "sp")}
    y_prompt = y_sample = None
    kv_bufs = (jnp.zeros((DEPTH, ROWS, 512), F32), jnp.zeros((DEPTH, ROWS, 512), F32))
    for l in range(DEPTH):
        pv, q, k, v, kp, qt3, vt3 = _inproj(h, norm_mix[l][None], w_in[l].astype(BF16), perm,
                                            l, kv_bufs)
        kv_bufs = (k, v)

        sb_p = _attn_prompt(kp, qt3, vt3)
        sb_s = _attn_sample(cache_k[l].reshape(DEC_BATCH, PAST_LEN, SB_WIDTH),
                            cache_v[l].reshape(DEC_BATCH, PAST_LEN, SB_WIDTH),
                            k, v, _sample_qbd(q), l)

        pw = pool_w[l].astype(BF16)
        ps = pool_scale[l][None]
        pool_p = _pool(pv, pw, ps, rows=ROWS_P, tm=TILE, use_halo=True)
        pv_s = pv[ROWS_P:].reshape(DEC_BATCH, DEC_SEQ, POOL_WIDTH)
        pbuf = jnp.concatenate([jnp.zeros((DEC_BATCH, 1, POOL_WIDTH), F32), state_pool[l], pv_s],
                               axis=1).reshape(DEC_BATCH * 2 * DEC_SEQ, POOL_WIDTH)
        pool_s = _pool(pbuf, pw, ps, rows=DEC_BATCH * 2 * DEC_SEQ, tm=TM, use_halo=False)
        pool_s = pool_s.reshape(DEC_BATCH, 2 * DEC_SEQ, POOL_WIDTH)[:, DEC_SEQ:].reshape(ROWS_S, POOL_WIDTH)

        wo = w_out[l].astype(BF16)
        j = l // 2
        if l % 2 == 0:
            hmid, hn = _mix(h, pool_p, pool_s, sb_p, sb_s, wo, norm_ffn[l][None])
            h = _ffn(hmid, hn, ffn_w_gate[j].astype(BF16), ffn_w_up[j].astype(BF16),
                     ffn_w_down[j].astype(BF16))
        else:
            router = jnp.pad(moe_router[j], ((0, 0), (0, LANES - N_EXPERTS)))
            hmid, hn, route = _mix(h, pool_p, pool_s, sb_p, sb_s, wo, norm_ffn[l][None], router)
            y_prompt, y_sample = _moe(hmid, hn, route, moe_w_gate[j].astype(BF16),
                                      moe_w_up[j].astype(BF16), moe_w_down[j].astype(BF16),
                                      final_norm[None])

        outs["pp"].append(pv[:ROWS_P].reshape(BATCH, SEQ_PAD, POOL_WIDTH)[:, -POOL_STATE:])
        outs["sp"].append(pv_s[:, -POOL_STATE:])

    def prompt_rows(buf):
        rows = buf[:, :ROWS_P].reshape(DEPTH, BATCH, SEQ_PAD, SB_HEADS, SB_HEAD_DIM)
        return rows[:, :, FRONT:]

    def sample_rows(buf):
        return buf[:, ROWS_P:].reshape(DEPTH, DEC_BATCH, DEC_SEQ, SB_HEADS, SB_HEAD_DIM)

    k, v = kv_bufs
    y_sample = y_sample.reshape(DEC_BATCH, DEC_SEQ, D_MODEL)
    return (y_prompt, y_sample, prompt_rows(k), prompt_rows(v), jnp.stack(outs["pp"]),
            sample_rows(k), sample_rows(v), jnp.stack(outs["sp"]))
```

```python
import functools

import jax
import jax.numpy as jnp
from jax import lax
from jax.experimental import pallas as pl
from jax.experimental.pallas import tpu as pltpu

F32 = jnp.float32
BF16 = jnp.bfloat16
I32 = jnp.int32

D_MODEL = 1024
BATCH = 2
SEQ = 8192
DEPTH = 2
DEC_BATCH = 32
DEC_SEQ = 16
PAST_LEN = 1024
N_META = 16
POOL_WIDTH = 512
POOL_WINDOWS = (2, 4, 8, 16)
POOL_GROUP_DIM = 128
POOL_STATE = 15
SB_HEADS = 8
SB_HEAD_DIM = 64
SB_WIDTH = 512
IN_WIDTH = 2048
D_FF = 2816
N_EXPERTS = 8
D_EXPERT = 3584
EPS = 1e-6

LANES = 128
SUBLANES = 8
TILE = 256
SEG = TILE // SUBLANES
FRONT = TILE - N_META
SEQ_PAD = FRONT + N_META + SEQ
SEQ_TILES = SEQ_PAD // TILE
ROWS_P = BATCH * SEQ_PAD
ROWS_S = DEC_BATCH * DEC_SEQ
ROWS = ROWS_P + ROWS_S
TM = 512
HALO = 16
VMEM_LIMIT = 56 * 1024 * 1024

TM_E = 512
PAIRS = 2 * ROWS
P_MAX = PAIRS + N_EXPERTS * TM_E
NT_E = P_MAX // TM_E
SLOT_ROWS = PAIRS + 2 * TM_E
EX_CHUNK = 1792


def _params(sem, vmem=VMEM_LIMIT):
    return pltpu.CompilerParams(dimension_semantics=sem, vmem_limit_bytes=vmem)


def _rms(x, gain):
    ms = jnp.mean(x * x, axis=-1, keepdims=True)
    return x * lax.rsqrt(ms + EPS) * gain


def _inproj_kernel(*refs):
    x_ref, g_ref, w_ref, perm_ref = refs[:4]
    pv_ref, q_ref, k_ref, v_ref, kp_ref, qt_ref, vt_ref = refs[-7:]
    hn = _rms(x_ref[...], g_ref[...]).astype(BF16)
    pv_ref[...] = jnp.dot(hn, w_ref[:, 0:512], preferred_element_type=F32)
    q = jnp.dot(hn, w_ref[:, 512:1024], preferred_element_type=F32) * (SB_HEAD_DIM ** -0.5)
    k = jnp.dot(hn, w_ref[:, 1024:1536], preferred_element_type=F32)
    v = jnp.dot(hn, w_ref[:, 1536:2048], preferred_element_type=F32)
    q_ref[...] = q.astype(BF16)
    k_ref[...] = k
    v_ref[...] = v
    perm = perm_ref[...]
    for half in range(TM // TILE):
        sl = slice(half * TILE, (half + 1) * TILE)
        kp_ref[sl, :] = jnp.dot(perm, k[sl].astype(BF16), preferred_element_type=F32).astype(BF16)
        vp = jnp.dot(perm, v[sl].astype(BF16), preferred_element_type=F32)
        vt_ref[half] = vp.T.astype(BF16)
        qt_ref[half] = (q[sl] * 0.5).T.astype(BF16)


def _inproj(h, gain, w, perm, layer, kv_bufs):
    row = lambda i: (i, 0)
    fixed = lambda i: (0, 0)
    tiles = lambda i: (i, 0, 0)
    lrow = lambda i: (layer, i, 0)
    per = TM // TILE
    in_specs = [pl.BlockSpec((TM, D_MODEL), row),
                pl.BlockSpec((1, D_MODEL), fixed),
                pl.BlockSpec((D_MODEL, IN_WIDTH), fixed),
                pl.BlockSpec((TILE, TILE), fixed),
                pl.BlockSpec(memory_space=pl.ANY), pl.BlockSpec(memory_space=pl.ANY)]
    args = [h, gain, w, perm, *kv_bufs]
    aliases = {4: 2, 5: 3}
    return pl.pallas_call(
        _inproj_kernel,
        grid=(ROWS // TM,),
        in_specs=in_specs,
        out_specs=[pl.BlockSpec((TM, 512), row), pl.BlockSpec((TM, 512), row),
                   pl.BlockSpec((None, TM, 512), lrow), pl.BlockSpec((None, TM, 512), lrow),
                   pl.BlockSpec((TM, 512), row)]
        + [pl.BlockSpec((per, SB_WIDTH, TILE), tiles)] * 2,
        out_shape=[jax.ShapeDtypeStruct((ROWS, 512), F32),
                   jax.ShapeDtypeStruct((ROWS, 512), BF16),
                   jax.ShapeDtypeStruct((DEPTH, ROWS, 512), F32),
                   jax.ShapeDtypeStruct((DEPTH, ROWS, 512), F32),
                   jax.ShapeDtypeStruct((ROWS, 512), BF16),
                   jax.ShapeDtypeStruct((ROWS // TILE, SB_WIDTH, TILE), BF16),
                   jax.ShapeDtypeStruct((ROWS // TILE, SB_WIDTH, TILE), BF16)],
        input_output_aliases=aliases,
        compiler_params=_params(("arbitrary",)),
        name="inproj",
    )(*args)


def _key_permutation():
    r = jnp.arange(TILE)
    src = (r % SUBLANES) * SEG + r // SUBLANES
    return (src[:, None] == jnp.arange(TILE)[None, :]).astype(BF16)


def _sb_half(zh, carry, lane0, masked):
    om = 0.5 - 0.5 * jnp.tanh(zh)
    if masked:
        r = lax.broadcasted_iota(I32, (TILE, LANES), 0)
        c = lax.broadcasted_iota(I32, (TILE, LANES), 1) + lane0
        key = ((r & (SUBLANES - 1)) * SEG) + (r >> 3)
        om = jnp.where(key < c, om, 1.0)
    run = jnp.ones((SUBLANES, LANES), F32)
    parts = [None] * SEG
    for a in reversed(range(SEG)):
        nxt = run * om[a * SUBLANES:(a + 1) * SUBLANES]
        parts[a] = run - nxt
        run = nxt
    sub = lax.broadcasted_iota(I32, (SUBLANES, LANES), 0)
    y = run
    for k in (1, 2, 4):
        y = y * jnp.where(sub + k < SUBLANES, pltpu.roll(y, SUBLANES - k, 0), 1.0)
    off = carry * jnp.where(sub + 1 < SUBLANES, pltpu.roll(y, SUBLANES - 1, 0), 1.0)
    new_carry = carry * jnp.broadcast_to(y[0:1, :], (SUBLANES, LANES))
    a_t = jnp.concatenate([p * off for p in parts], axis=0).astype(BF16)
    return a_t, new_carry


def _sb_tile(z, carry, masked):
    outs = []
    carries = []
    for lh in range(TILE // LANES):
        ls = slice(lh * LANES, (lh + 1) * LANES)
        a, c = _sb_half(z[:, ls], carry[:, ls], lh * LANES, masked)
        outs.append(a)
        carries.append(c)
    return jnp.concatenate(outs, axis=1), jnp.concatenate(carries, axis=1)


def _attn_prompt_kernel(nt_ref, k_ref, qt_ref, vt_ref, o_ref, zbuf, abuf):
    half = SB_HEAD_DIM
    row = lax.broadcasted_iota(I32, (LANES, TILE), 0)

    def k_tile(j):
        return k_ref[pl.ds(pl.multiple_of(j * TILE, TILE), TILE), :]

    def q_tile(i, _):
        qt = qt_ref[i].astype(F32)
        qh = (jnp.where(row < half, qt, 0.0).astype(BF16),
              jnp.where(row >= half, qt, 0.0).astype(BF16))
        ones = jnp.ones((SUBLANES, TILE), F32)

        kd = k_tile(i)
        carries = []
        for h in range(2):
            z = jnp.dot(kd, qh[h], preferred_element_type=F32)
            a, c = _sb_tile(z, ones, True)
            abuf[0, h] = a
            abuf[1, h] = jnp.zeros((TILE, TILE), BF16)
            carries.append(c)
        k0 = k_tile(jnp.maximum(i - 1, 0))
        k1 = k_tile(jnp.maximum(i - 2, 0))
        for h in range(2):
            zbuf[0, h] = jnp.dot(k0, qh[h], preferred_element_type=F32)
            zbuf[1, h] = jnp.dot(k1, qh[h], preferred_element_type=F32)

        def weighted_values(acc, v0, v1):
            out = []
            for h in range(2):
                rows = slice(h * half, (h + 1) * half)
                out.append(acc[h]
                           + jnp.dot(v0[rows], abuf[0, h], preferred_element_type=F32)
                           + jnp.dot(v1[rows], abuf[1, h], preferred_element_type=F32))
            return out

        def trip(st):
            t, _, c0, c1, acc0, acc1, p0, p1 = st
            j0 = i - 1 - 2 * t
            j1 = jnp.maximum(j0 - 1, 0)
            acc = weighted_values((acc0, acc1), vt_ref[p0], vt_ref[p1])
            cs = [c0, c1]
            for h in range(2):
                for s in range(2):
                    a, cs[h] = _sb_tile(zbuf[s, h], cs[h], False)
                    abuf[s, h] = a
            n0 = k_tile(jnp.maximum(j0 - 2, 0))
            n1 = k_tile(jnp.maximum(j0 - 3, 0))
            for h in range(2):
                zbuf[0, h] = jnp.dot(n0, qh[h], preferred_element_type=F32)
                zbuf[1, h] = jnp.dot(n1, qh[h], preferred_element_type=F32)
            alive = jnp.maximum(jnp.max(cs[0]), jnp.max(cs[1])) > 0.0
            return t + 1, alive, cs[0], cs[1], acc[0], acc[1], j0, j1

        zeros = jnp.zeros((half, TILE), F32)
        trips = (i + 1) // 2
        st = lax.while_loop(lambda s: (s[0] < trips) & s[1], trip,
                            (jnp.int32(0), jnp.bool_(True), carries[0], carries[1],
                             zeros, zeros, i, i))
        keep1 = (i == 0) | ((i & 1) == 0) | (st[0] < trips)
        v1 = jnp.where(keep1, vt_ref[st[7]].astype(F32), 0.0).astype(BF16)
        acc = weighted_values((st[4], st[5]), vt_ref[st[6]], v1)
        out_t = jnp.concatenate(acc, axis=0)
        o_ref[pl.ds(pl.multiple_of(i * TILE, TILE), TILE), :] = out_t.T.astype(BF16)
        return 0

    lax.fori_loop(0, nt_ref[0], q_tile, 0)


def _attn_prompt(kp, qt3, vt3, *, batch=BATCH, n_tiles=SEQ_TILES):
    rows = n_tiles * TILE
    return pl.pallas_call(
        _attn_prompt_kernel,
        grid=(batch, SB_HEADS // 2),
        in_specs=[pl.BlockSpec(memory_space=pltpu.SMEM),
                  pl.BlockSpec((rows, LANES), lambda p, g: (p, g)),
                  pl.BlockSpec((n_tiles, LANES, TILE), lambda p, g: (p, g, 0)),
                  pl.BlockSpec((n_tiles, LANES, TILE), lambda p, g: (p, g, 0))],
        out_specs=pl.BlockSpec((rows, LANES), lambda p, g: (p, g)),
        out_shape=jax.ShapeDtypeStruct((batch * rows, SB_WIDTH), BF16),
        scratch_shapes=[pltpu.VMEM((2, 2, TILE, TILE), F32),
                        pltpu.VMEM((2, 2, TILE, TILE), BF16)],
        compiler_params=_params(("arbitrary", "arbitrary")),
        name="attn_prompt",
    )(jnp.full((1,), n_tiles, I32), kp, qt3, vt3)


def _rev_excl_cumprod(om, scr):
    n = om.shape[0]
    sub = lax.broadcasted_iota(I32, (n, LANES), 0) & (SUBLANES - 1)
    scr[n:n + SUBLANES, :] = jnp.ones((SUBLANES, LANES), F32)
    y = om
    for k in (1, 2, 4):
        scr[0:n, :] = y
        y = y * jnp.where(sub + k < SUBLANES, scr[k:n + k, :], 1.0)
    scr[0:n, :] = y
    ex = jnp.where(sub + 1 < SUBLANES, scr[1:n + 1, :], 1.0)
    run = jnp.ones((SUBLANES, LANES), F32)
    out = [None] * (n // SUBLANES)
    for a in reversed(range(n // SUBLANES)):
        sl = slice(a * SUBLANES, (a + 1) * SUBLANES)
        out[a] = ex[sl] * run
        run = run * jnp.broadcast_to(y[a * SUBLANES:a * SUBLANES + 1, :], (SUBLANES, LANES))
    return jnp.concatenate(out, axis=0)


def _attn_sample_kernel(ck_ref, cv_ref, kn_ref, vn_ref, qbd_ref, o_ref, scr):
    qbd = qbd_ref[0]
    pad = jnp.zeros((LANES - DEC_SEQ, SB_WIDTH), F32)
    kn = jnp.concatenate([kn_ref[...], pad], axis=0).astype(BF16)
    vn = jnp.concatenate([vn_ref[...], pad], axis=0).astype(BF16)
    zc = jnp.dot(ck_ref[0].astype(BF16), qbd, preferred_element_type=F32)
    zn = jnp.dot(kn, qbd, preferred_element_type=F32)
    z = jnp.concatenate([zc, zn], axis=0)
    n = PAST_LEN + LANES
    r = lax.broadcasted_iota(I32, (n, LANES), 0)
    c = lax.broadcasted_iota(I32, (n, LANES), 1)
    valid = (r < PAST_LEN) | ((r - PAST_LEN) < (c & (DEC_SEQ - 1)))
    e = jnp.exp(z)
    om = 1.0 / (1.0 + e)
    beta = jnp.where(valid, 1.0 - om, 0.0)
    om = jnp.where(valid, om, 1.0)
    a_t = beta * _rev_excl_cumprod(om, scr)
    ac = a_t[0:PAST_LEN].T.astype(BF16)
    an = a_t[PAST_LEN:n].T.astype(BF16)
    p = jnp.dot(ac, cv_ref[0].astype(BF16), preferred_element_type=F32)
    p = p + jnp.dot(an, vn, preferred_element_type=F32)
    pr = lax.broadcasted_iota(I32, (LANES, SB_WIDTH), 0)
    pc = lax.broadcasted_iota(I32, (LANES, SB_WIDTH), 1)
    p = jnp.where((pr >> 4) == (pc >> 6), p, 0.0)
    out = p[0:DEC_SEQ]
    for h in range(1, SB_HEADS):
        out = out + p[h * DEC_SEQ:(h + 1) * DEC_SEQ]
    o_ref[...] = out.astype(BF16)


def _attn_sample(ck, cv, k, v, qbd, layer):
    new = lambda s: (layer, ROWS_P // DEC_SEQ + s, 0)
    per = lambda s: (s, 0, 0)
    return pl.pallas_call(
        _attn_sample_kernel,
        grid=(DEC_BATCH,),
        in_specs=[pl.BlockSpec((1, PAST_LEN, SB_WIDTH), per),
                  pl.BlockSpec((1, PAST_LEN, SB_WIDTH), per),
                  pl.BlockSpec((None, DEC_SEQ, SB_WIDTH), new),
                  pl.BlockSpec((None, DEC_SEQ, SB_WIDTH), new),
                  pl.BlockSpec((1, SB_WIDTH, LANES), per)],
        out_specs=pl.BlockSpec((DEC_SEQ, SB_WIDTH), lambda s: (s, 0)),
        out_shape=jax.ShapeDtypeStruct((ROWS_S, SB_WIDTH), BF16),
        scratch_shapes=[pltpu.VMEM((PAST_LEN + LANES + SUBLANES, LANES), F32)],
        compiler_params=_params(("arbitrary",)),
        name="attn_sample",
    )(ck, cv, k, v, qbd)


def _pool_kernel(*refs, tm, use_halo):
    if use_halo:
        pv_ref, halo_ref, w_ref, sc_ref, o_ref, s1, s2, s4, s8 = refs
    else:
        pv_ref, w_ref, sc_ref, o_ref, s1, s2, s4, s8 = refs
    i = pl.program_id(0)
    off = SUBLANES
    n = HALO + tm
    x = pv_ref[...]
    zero8 = jnp.zeros((off, POOL_WIDTH), F32)
    if use_halo:
        first = (i % (SEQ_PAD // tm)) == 0
        halo = jnp.where(first, 0.0, halo_ref[...])
    else:
        halo = jnp.zeros((HALO, POOL_WIDTH), F32)
    s1[0:off, :] = zero8
    s1[off:off + HALO, :] = halo
    s1[off + HALO:off + n, :] = x
    g = POOL_GROUP_DIM
    t2 = s1[off:off + n, :] + s1[off - 1:off - 1 + n, :]
    s2[0:off, :] = zero8[:, g:]
    s2[off:off + n, :] = t2[:, g:]
    t4 = t2[:, g:] + s2[off - 2:off - 2 + n, :]
    s4[0:off, :] = zero8[:, 2 * g:]
    s4[off:off + n, :] = t4[:, g:]
    t8 = t4[:, g:] + s4[off - 4:off - 4 + n, :]
    s8[0:off, :] = zero8[:, 3 * g:]
    s8[off:off + n, :] = t8[:, g:]
    t16 = t8[:, g:] + s8[off - 8:off - 8 + n, :]
    sums = (t2[HALO:, 0:g], t4[HALO:, 0:g], t8[HALO:, 0:g], t16[HALO:, :])
    if use_halo:
        idx = (i % (SEQ_PAD // tm)) * tm - FRONT + lax.broadcasted_iota(I32, (tm, g), 0)
    for gi, win in enumerate(POOL_WINDOWS):
        if use_halo:
            cnt = jnp.clip(idx + 1, 1, win).astype(F32)
        else:
            cnt = jnp.full((tm, g), float(win), F32)
        d = sums[gi] / cnt - x[:, gi * g:(gi + 1) * g]
        y = jnp.dot(d.astype(BF16), w_ref[gi], preferred_element_type=F32)
        o_ref[:, gi * g:(gi + 1) * g] = (y * sc_ref[:, gi * g:(gi + 1) * g]).astype(BF16)


def _pool(pv, w, scale, *, rows, tm, use_halo):
    g = POOL_GROUP_DIM
    in_specs = [pl.BlockSpec((tm, POOL_WIDTH), lambda i: (i, 0))]
    args = [pv]
    if use_halo:
        in_specs.append(pl.BlockSpec((HALO, POOL_WIDTH),
                                     lambda i: (jnp.maximum(i * (tm // HALO) - 1, 0), 0)))
        args.append(pv)
    in_specs += [pl.BlockSpec((4, g, g), lambda i: (0, 0, 0)),
                 pl.BlockSpec((1, POOL_WIDTH), lambda i: (0, 0))]
    args += [w, scale]
    n = SUBLANES + HALO + tm
    return pl.pallas_call(
        functools.partial(_pool_kernel, tm=tm, use_halo=use_halo),
        grid=(rows // tm,),
        in_specs=in_specs,
        out_specs=pl.BlockSpec((tm, POOL_WIDTH), lambda i: (i, 0)),
        out_shape=jax.ShapeDtypeStruct((rows, POOL_WIDTH), BF16),
        scratch_shapes=[pltpu.VMEM((n, 4 * g), F32), pltpu.VMEM((n, 3 * g), F32),
                        pltpu.VMEM((n, 2 * g), F32), pltpu.VMEM((n, g), F32)],
        compiler_params=_params(("arbitrary",)),
        name="pool_prompt" if use_halo else "pool_sample",
    )(*args)


def _mix_kernel(*refs, route):
    if route:
        (h_ref, pp_ref, ps_ref, sp_ref, ss_ref, w_ref, g_ref, r_ref,
         hmid_ref, hn_ref, rt_ref) = refs
    else:
        h_ref, pp_ref, ps_ref, sp_ref, ss_ref, w_ref, g_ref, hmid_ref, hn_ref = refs
    prompt = pl.program_id(0) < ROWS_P // TM
    pool = jnp.where(prompt, pp_ref[...], ps_ref[...])
    sb = jnp.where(prompt, sp_ref[...], ss_ref[...])
    hmid = (h_ref[...]
            + jnp.dot(pool, w_ref[0:POOL_WIDTH, :], preferred_element_type=F32)
            + jnp.dot(sb, w_ref[POOL_WIDTH:, :], preferred_element_type=F32))
    hmid_ref[...] = hmid
    hn = _rms(hmid, g_ref[...])
    hn_ref[...] = hn.astype(hn_ref.dtype)
    if route:
        tm = hn.shape[0]
        h_hi = hn.astype(BF16)
        h_lo = (hn - h_hi.astype(F32)).astype(BF16)
        r = r_ref[...]
        r_hi = r.astype(BF16)
        r_lo = (r - r_hi.astype(F32)).astype(BF16)
        logits = (jnp.dot(h_hi, r_hi, preferred_element_type=F32)
                  + jnp.dot(h_hi, r_lo, preferred_element_type=F32)
                  + jnp.dot(h_lo, r_hi, preferred_element_type=F32))
        lane = lax.broadcasted_iota(I32, (tm, LANES), 1)
        neg = jnp.float32(-jnp.inf)
        logits = jnp.where(lane < N_EXPERTS, logits, neg)
        m1 = jnp.max(logits, axis=-1, keepdims=True)
        i1 = jnp.min(jnp.where(logits == m1, lane, LANES), axis=-1, keepdims=True)
        rest = jnp.where(lane == i1, neg, logits)
        m2 = jnp.max(rest, axis=-1, keepdims=True)
        i2 = jnp.min(jnp.where(rest == m2, lane, LANES), axis=-1, keepdims=True)
        t = jnp.exp(m2 - m1)
        g1 = 1.0 / (1.0 + t)
        g2 = t / (1.0 + t)
        rt_ref[...] = (jnp.where(lane == 0, i1.astype(F32), 0.0)
                       + jnp.where(lane == 1, i2.astype(F32), 0.0)
                       + jnp.where(lane == 2, g1, 0.0) + jnp.where(lane == 3, g2, 0.0))


def _mix(h, pool_p, pool_s, sb_p, sb_s, w, gain, router=None):
    row = lambda i: (i, 0)
    fixed = lambda i: (0, 0)
    prow = lambda i: (jnp.minimum(i, ROWS_P // TM - 1), 0)
    route = router is not None
    in_specs = [pl.BlockSpec((TM, D_MODEL), row),
                pl.BlockSpec((TM, POOL_WIDTH), prow),
                pl.BlockSpec((TM, POOL_WIDTH), fixed),
                pl.BlockSpec((TM, SB_WIDTH), prow),
                pl.BlockSpec((TM, SB_WIDTH), fixed),
                pl.BlockSpec((D_MODEL, D_MODEL), fixed),
                pl.BlockSpec((1, D_MODEL), fixed)]
    out_specs = [pl.BlockSpec((TM, D_MODEL), row), pl.BlockSpec((TM, D_MODEL), row)]
    out_shape = [jax.ShapeDtypeStruct((ROWS, D_MODEL), F32),
                 jax.ShapeDtypeStruct((ROWS, D_MODEL), F32 if route else BF16)]
    args = [h, pool_p, pool_s, sb_p, sb_s, w, gain]
    if route:
        in_specs.append(pl.BlockSpec((D_MODEL, LANES), fixed))
        out_specs.append(pl.BlockSpec((TM, LANES), row))
        out_shape.append(jax.ShapeDtypeStruct((ROWS, LANES), F32))
        args.append(router)
    return pl.pallas_call(
        functools.partial(_mix_kernel, route=route),
        grid=(ROWS // TM,),
        in_specs=in_specs, out_specs=out_specs, out_shape=out_shape,
        compiler_params=_params(("arbitrary",)),
        name="mix_route" if route else "mix",
    )(*args)


def _silu_mul(g, u):
    return g * (1.0 / (1.0 + jnp.exp(-g))) * u


FF_CHUNK = 256


def _ffn_kernel(hmid_ref, hn_ref, wg_ref, wu_ref, wd_ref, o_ref, act):
    hn = hn_ref[...]
    for c in range(D_FF // FF_CHUNK):
        sl = slice(c * FF_CHUNK, (c + 1) * FF_CHUNK)
        g = jnp.dot(hn, wg_ref[:, sl], preferred_element_type=F32)
        u = jnp.dot(hn, wu_ref[:, sl], preferred_element_type=F32)
        act[:, sl] = _silu_mul(g, u).astype(BF16)
    o_ref[...] = hmid_ref[...] + jnp.dot(act[...], wd_ref[...], preferred_element_type=F32)


def _ffn(hmid, hn, wg, wu, wd):
    row = lambda i: (i, 0)
    fixed = lambda i: (0, 0)
    once = pl.Buffered(1)
    return pl.pallas_call(
        _ffn_kernel,
        grid=(ROWS // TM,),
        in_specs=[pl.BlockSpec((TM, D_MODEL), row),
                  pl.BlockSpec((TM, D_MODEL), row),
                  pl.BlockSpec((D_MODEL, D_FF), fixed, pipeline_mode=once),
                  pl.BlockSpec((D_MODEL, D_FF), fixed, pipeline_mode=once),
                  pl.BlockSpec((D_FF, D_MODEL), fixed, pipeline_mode=once)],
        out_specs=pl.BlockSpec((TM, D_MODEL), row),
        out_shape=jax.ShapeDtypeStruct((ROWS, D_MODEL), F32),
        scratch_shapes=[pltpu.VMEM((TM, D_FF), BF16)],
        compiler_params=_params(("arbitrary",)),
        name="ffn",
    )(hmid, hn, wg, wu, wd)


def _route_plan(route):
    e_flat = jnp.concatenate([route[:, 0], route[:, 1]]).astype(I32)
    order = jnp.argsort(e_flat).astype(I32)
    ex = jnp.arange(N_EXPERTS, dtype=I32)
    counts = jnp.sum((e_flat[:, None] == ex[None, :]).astype(I32), axis=0)
    gstart = jnp.cumsum(counts) - counts
    padded = ((counts + TM_E - 1) // TM_E) * TM_E
    pend = jnp.cumsum(padded)
    pstart = pend - padded
    q = jnp.arange(P_MAX, dtype=I32)
    e_q = jnp.sum((q[:, None] >= pend[None, :]).astype(I32), axis=1)
    e_c = jnp.minimum(e_q, N_EXPERTS - 1)
    within = q - pstart[e_c]
    valid = (e_q < N_EXPERTS) & (within < counts[e_c])
    pair = order[jnp.clip(gstart[e_c] + within, 0, PAIRS - 1)]
    src = jnp.where(valid, jnp.where(pair >= ROWS, pair - ROWS, pair), 0)
    spare = PAIRS + ((q // TM_E) % 2) * TM_E + (q % TM_E)
    dst = jnp.where(valid, pair, spare)
    tile_expert = e_c[::TM_E]
    n_used = (pend[-1] // TM_E).reshape(1)
    return (src.reshape(NT_E, 1, TM_E), dst.reshape(NT_E, 1, TM_E), tile_expert, n_used)


DMA_UNROLL = 8


def _row_dmas(per, make):
    def issue(r8, _):
        for u in range(DMA_UNROLL):
            make(r8 * DMA_UNROLL + u).start(priority=u % 2)
        return 0

    lax.fori_loop(0, per // DMA_UNROLL, issue, 0)


def _gather_kernel(nu_ref, idx_ref, src_ref, out_ref, sem):
    per = out_ref.shape[0]
    used = pl.program_id(0) < nu_ref[0]

    @pl.when(used)
    def _():
        _row_dmas(per, lambda r: pltpu.make_async_copy(
            src_ref.at[pl.ds(idx_ref[0, 0, r], 1)], out_ref.at[pl.ds(r, 1)], sem))
        pltpu.make_async_copy(src_ref.at[pl.ds(0, per)], out_ref, sem).wait()

    @pl.when(jnp.logical_not(used))
    def _():
        out_ref[...] = jnp.zeros(out_ref.shape, out_ref.dtype)


def _gather_rows(n_used, idx, src):
    n_steps, _, per = idx.shape
    grid_spec = pltpu.PrefetchScalarGridSpec(
        num_scalar_prefetch=1,
        grid=(n_steps,),
        in_specs=[pl.BlockSpec((1, 1, per), lambda s, nu: (s, 0, 0), memory_space=pltpu.SMEM),
                  pl.BlockSpec(memory_space=pl.ANY)],
        out_specs=pl.BlockSpec((per, src.shape[1]), lambda s, nu: (s, 0)),
        scratch_shapes=[pltpu.SemaphoreType.DMA(())])
    return pl.pallas_call(
        _gather_kernel,
        grid_spec=grid_spec,
        out_shape=jax.ShapeDtypeStruct((n_steps * per, src.shape[1]), src.dtype),
        compiler_params=_params(("arbitrary",)),
        name="moe_gather",
    )(n_used, idx, src)


def _scatter_kernel(nu_ref, idx_ref, in_ref, dst_ref, zeros, sem):
    per = in_ref.shape[0]

    @pl.when(pl.program_id(0) == 0)
    def _():
        zeros[...] = jnp.zeros(zeros.shape, zeros.dtype)
        spare = [pltpu.make_async_copy(zeros, dst_ref.at[pl.ds(PAIRS + k * per, per)], sem)
                 for k in range((dst_ref.shape[0] - PAIRS) // per)]
        for cp in spare:
            cp.start()
        for cp in spare:
            cp.wait()

    @pl.when(pl.program_id(0) < nu_ref[0])
    def _():
        _row_dmas(per, lambda r: pltpu.make_async_copy(
            in_ref.at[pl.ds(r, 1)], dst_ref.at[pl.ds(idx_ref[0, 0, r], 1)], sem))
        pltpu.make_async_copy(in_ref, dst_ref.at[pl.ds(0, per)], sem).wait()


def _scatter_rows(n_used, idx, src, out_rows):
    n_steps, _, per = idx.shape
    held = lambda s, nu: (jnp.minimum(s, nu[0] - 1), 0)
    grid_spec = pltpu.PrefetchScalarGridSpec(
        num_scalar_prefetch=1,
        grid=(n_steps,),
        in_specs=[pl.BlockSpec((1, 1, per), lambda s, nu: (s, 0, 0), memory_space=pltpu.SMEM),
                  pl.BlockSpec((per, src.shape[1]), held)],
        out_specs=pl.BlockSpec(memory_space=pl.ANY),
        scratch_shapes=[pltpu.VMEM((per, src.shape[1]), src.dtype), pltpu.SemaphoreType.DMA(())])
    return pl.pallas_call(
        _scatter_kernel,
        grid_spec=grid_spec,
        out_shape=jax.ShapeDtypeStruct((out_rows, src.shape[1]), src.dtype),
        compiler_params=_params(("arbitrary",)),
        name="moe_scatter",
    )(n_used, idx, src)


def _experts_kernel(te_ref, nu_ref, x_ref, wg_ref, wu_ref, wd_ref, y_ref, act):
    t = pl.program_id(0)
    c = pl.program_id(1)

    @pl.when(t < nu_ref[0])
    def _():
        x = x_ref[...].astype(BF16)
        for s in range(EX_CHUNK // FF_CHUNK):
            sl = slice(s * FF_CHUNK, (s + 1) * FF_CHUNK)
            g = jnp.dot(x, wg_ref[0, :, sl], preferred_element_type=F32)
            u = jnp.dot(x, wu_ref[0, :, sl], preferred_element_type=F32)
            act[:, sl] = _silu_mul(g, u).astype(BF16)
        y = jnp.dot(act[...], wd_ref[0], preferred_element_type=F32)

        @pl.when(c == 0)
        def _():
            y_ref[...] = y

        @pl.when(c > 0)
        def _():
            y_ref[...] += y

    @pl.when((t >= nu_ref[0]) & (c == 0))
    def _():
        y_ref[...] = jnp.zeros(y_ref.shape, y_ref.dtype)


def _experts(tile_expert, n_used, xs, wg, wu, wd):
    n_chunks = D_EXPERT // EX_CHUNK

    def held(t, c, nu):
        live = t < nu[0]
        return jnp.where(live, t, nu[0] - 1), jnp.where(live, c, n_chunks - 1)

    def x_map(t, c, te, nu):
        return held(t, c, nu)[0], 0

    def w_in_map(t, c, te, nu):
        tt, cc = held(t, c, nu)
        return te[tt], 0, cc

    def w_out_map(t, c, te, nu):
        tt, cc = held(t, c, nu)
        return te[tt], cc, 0

    grid_spec = pltpu.PrefetchScalarGridSpec(
        num_scalar_prefetch=2,
        grid=(NT_E, n_chunks),
        in_specs=[pl.BlockSpec((TM_E, D_MODEL), x_map),
                  pl.BlockSpec((1, D_MODEL, EX_CHUNK), w_in_map),
                  pl.BlockSpec((1, D_MODEL, EX_CHUNK), w_in_map),
                  pl.BlockSpec((1, EX_CHUNK, D_MODEL), w_out_map)],
        out_specs=pl.BlockSpec((TM_E, D_MODEL), lambda t, c, te, nu: (t, 0)),
        scratch_shapes=[pltpu.VMEM((TM_E, EX_CHUNK), BF16)])
    return pl.pallas_call(
        _experts_kernel,
        grid_spec=grid_spec,
        out_shape=jax.ShapeDtypeStruct((P_MAX, D_MODEL), F32),
        compiler_params=_params(("arbitrary", "arbitrary")),
        name="experts",
    )(tile_expert, n_used, xs, wg, wu, wd)


def _combine_kernel(hmid_ref, a_ref, b_ref, rt_ref, fg_ref, yp_ref, ys_ref):
    i = pl.program_id(0)
    lane = lax.broadcasted_iota(I32, rt_ref.shape, 1)
    rt = rt_ref[...]
    g1 = jnp.sum(jnp.where(lane == 2, rt, 0.0), axis=-1, keepdims=True)
    g2 = jnp.sum(jnp.where(lane == 3, rt, 0.0), axis=-1, keepdims=True)
    y = _rms(hmid_ref[...] + g1 * a_ref[...] + g2 * b_ref[...], fg_ref[...])
    prompt = i < ROWS_P // TILE

    @pl.when(prompt & (i % SEQ_TILES != 0))
    def _():
        yp_ref[0] = y

    @pl.when(jnp.logical_not(prompt))
    def _():
        ys_ref[...] = y


def _combine(hmid, slots, route, fgain):
    n_p = ROWS_P // TILE

    def yp_map(i):
        ip = jnp.minimum(i, n_p - 1)
        return (ip // SEQ_TILES, jnp.maximum(ip % SEQ_TILES - 1, 0), 0)

    return pl.pallas_call(
        _combine_kernel,
        grid=(ROWS // TILE,),
        in_specs=[pl.BlockSpec((TILE, D_MODEL), lambda i: (i, 0)),
                  pl.BlockSpec((TILE, D_MODEL), lambda i: (i, 0)),
                  pl.BlockSpec((TILE, D_MODEL), lambda i: (i + ROWS // TILE, 0)),
                  pl.BlockSpec((TILE, LANES), lambda i: (i, 0)),
                  pl.BlockSpec((1, D_MODEL), lambda i: (0, 0))],
        out_specs=[pl.BlockSpec((1, TILE, D_MODEL), yp_map),
                   pl.BlockSpec((TILE, D_MODEL), lambda i: (jnp.maximum(i - n_p, 0), 0))],
        out_shape=[jax.ShapeDtypeStruct((BATCH, SEQ, D_MODEL), F32),
                   jax.ShapeDtypeStruct((ROWS_S, D_MODEL), F32)],
        compiler_params=_params(("arbitrary",)),
        name="combine",
    )(hmid, slots, slots, route, fgain)


def _moe(hmid, hn, route, wg, wu, wd, fgain):
    src, dst, tile_expert, n_used = _route_plan(route)
    xs = _gather_rows(n_used, src, hn)
    ys = _experts(tile_expert, n_used, xs, wg, wu, wd)
    slots = _scatter_rows(n_used, dst, ys, SLOT_ROWS)
    return _combine(hmid, slots, route, fgain)


def _sample_qbd(q):
    qs = q[ROWS_P:].reshape(DEC_BATCH, DEC_SEQ, SB_HEADS, SB_HEAD_DIM)
    eye = jnp.eye(SB_HEADS, dtype=q.dtype)
    qbd = jnp.einsum('sthd,hg->shdgt', qs, eye)
    return qbd.reshape(DEC_BATCH, SB_WIDTH, SB_HEADS * DEC_SEQ)


def kernel(x_prompt, x_sample, cache_k, cache_v, state_pool, meta_tokens, norm_mix, w_in, pool_w, pool_scale, w_out, norm_ffn, ffn_w_gate, ffn_w_up, ffn_w_down, moe_router, moe_w_gate, moe_w_up, moe_w_down, final_norm):
    front = jnp.zeros((BATCH, FRONT, D_MODEL), F32)
    meta = jnp.broadcast_to(meta_tokens[None], (BATCH, N_META, D_MODEL))
    h = jnp.concatenate([
        jnp.concatenate([front, meta, x_prompt], axis=1).reshape(ROWS_P, D_MODEL),
        x_sample.reshape(ROWS_S, D_MODEL)], axis=0)
    perm = _key_permutation()

    outs = {name: [] for name in ("pp", "sp")}
    y_prompt = y_sample = None
    kv_bufs = (jnp.zeros((DEPTH, ROWS, 512), F32), jnp.zeros((DEPTH, ROWS, 512), F32))
    for l in range(DEPTH):
        pv, q, k, v, kp, qt3, vt3 = _inproj(h, norm_mix[l][None], w_in[l].astype(BF16), perm,
                                            l, kv_bufs)
        kv_bufs = (k, v)

        sb_p = _attn_prompt(kp, qt3, vt3)
        sb_s = _attn_sample(cache_k[l].reshape(DEC_BATCH, PAST_LEN, SB_WIDTH),
                            cache_v[l].reshape(DEC_BATCH, PAST_LEN, SB_WIDTH),
                            k, v, _sample_qbd(q), l)

        pw = pool_w[l].astype(BF16)
        ps = pool_scale[l][None]
        pool_p = _pool(pv, pw, ps, rows=ROWS_P, tm=TILE, use_halo=True)
        pv_s = pv[ROWS_P:].reshape(DEC_BATCH, DEC_SEQ, POOL_WIDTH)
        pbuf = jnp.concatenate([jnp.zeros((DEC_BATCH, 1, POOL_WIDTH), F32), state_pool[l], pv_s],
                               axis=1).reshape(DEC_BATCH * 2 * DEC_SEQ, POOL_WIDTH)
        pool_s = _pool(pbuf, pw, ps, rows=DEC_BATCH * 2 * DEC_SEQ, tm=TM, use_halo=False)
        pool_s = pool_s.reshape(DEC_BATCH, 2 * DEC_SEQ, POOL_WIDTH)[:, DEC_SEQ:].reshape(ROWS_S, POOL_WIDTH)

        wo = w_out[l].astype(BF16)
        j = l // 2
        if l % 2 == 0:
            hmid, hn = _mix(h, pool_p, pool_s, sb_p, sb_s, wo, norm_ffn[l][None])
            h = _ffn(hmid, hn, ffn_w_gate[j].astype(BF16), ffn_w_up[j].astype(BF16),
                     ffn_w_down[j].astype(BF16))
        else:
            router = jnp.pad(moe_router[j], ((0, 0), (0, LANES - N_EXPERTS)))
            hmid, hn, route = _mix(h, pool_p, pool_s, sb_p, sb_s, wo, norm_ffn[l][None], router)
            y_prompt, y_sample = _moe(hmid, hn, route, moe_w_gate[j].astype(BF16),
                                      moe_w_up[j].astype(BF16), moe_w_down[j].astype(BF16),
                                      final_norm[None])

        outs["pp"].append(pv[:ROWS_P].reshape(BATCH, SEQ_PAD, POOL_WIDTH)[:, -POOL_STATE:])
        outs["sp"].append(pv_s[:, -POOL_STATE:])

    def prompt_rows(buf):
        rows = buf[:, :ROWS_P].reshape(DEPTH, BATCH, SEQ_PAD, SB_HEADS, SB_HEAD_DIM)
        return rows[:, :, FRONT:]

    def sample_rows(buf):
        return buf[:, ROWS_P:].reshape(DEPTH, DEC_BATCH, DEC_SEQ, SB_HEADS, SB_HEAD_DIM)

    k, v = kv_bufs
    y_sample = y_sample.reshape(DEC_BATCH, DEC_SEQ, D_MODEL)
    return (y_prompt, y_sample, prompt_rows(k), prompt_rows(v), jnp.stack(outs["pp"]),
            sample_rows(k), sample_rows(v), jnp.stack(outs["sp"]))
```

```python
import functools

import jax
import jax.numpy as jnp
from jax import lax
from jax.experimental import pallas as pl
from jax.experimental.pallas import tpu as pltpu

F32 = jnp.float32
BF16 = jnp.bfloat16
I32 = jnp.int32

D_MODEL = 1024
BATCH = 2
SEQ = 8192
DEPTH = 2
DEC_BATCH = 32
DEC_SEQ = 16
PAST_LEN = 1024
N_META = 16
POOL_WIDTH = 512
POOL_WINDOWS = (2, 4, 8, 16)
POOL_GROUP_DIM = 128
POOL_STATE = 15
SB_HEADS = 8
SB_HEAD_DIM = 64
SB_WIDTH = 512
IN_WIDTH = 2048
D_FF = 2816
N_EXPERTS = 8
D_EXPERT = 3584
EPS = 1e-6

LANES = 128
SUBLANES = 8
TILE = 256
SEG = TILE // SUBLANES
FRONT = TILE - N_META
SEQ_PAD = FRONT + N_META + SEQ
SEQ_TILES = SEQ_PAD // TILE
ROWS_P = BATCH * SEQ_PAD
ROWS_S = DEC_BATCH * DEC_SEQ
ROWS = ROWS_P + ROWS_S
TM = 512
HALO = 16
VMEM_LIMIT = 56 * 1024 * 1024

TM_E = 512
PAIRS = 2 * ROWS
P_MAX = PAIRS + N_EXPERTS * TM_E
NT_E = P_MAX // TM_E
SLOT_ROWS = PAIRS + 2 * TM_E
EX_CHUNK = 1792


def _params(sem, vmem=VMEM_LIMIT):
    return pltpu.CompilerParams(dimension_semantics=sem, vmem_limit_bytes=vmem)


def _rms(x, gain):
    ms = jnp.mean(x * x, axis=-1, keepdims=True)
    return x * lax.rsqrt(ms + EPS) * gain


def _token_rows_kernel(xp_ref, meta_ref, xs_ref, o_ref):
    i = pl.program_id(0)
    t = i % SEQ_TILES
    prompt = i < ROWS_P // TILE

    @pl.when(prompt & (t == 0))
    def _():
        o_ref[...] = jnp.concatenate([jnp.zeros((FRONT, D_MODEL), F32), meta_ref[...]], axis=0)

    @pl.when(prompt & (t > 0))
    def _():
        o_ref[...] = xp_ref[0]

    @pl.when(jnp.logical_not(prompt))
    def _():
        o_ref[...] = xs_ref[...]


def _token_rows(x_prompt, meta, x_sample):
    n_p = ROWS_P // TILE

    def xp_map(i):
        ip = jnp.minimum(i, n_p - 1)
        return ip // SEQ_TILES, jnp.maximum(ip % SEQ_TILES - 1, 0), 0

    return pl.pallas_call(
        _token_rows_kernel,
        grid=(ROWS // TILE,),
        in_specs=[pl.BlockSpec((1, TILE, D_MODEL), xp_map),
                  pl.BlockSpec((N_META, D_MODEL), lambda i: (0, 0)),
                  pl.BlockSpec((TILE, D_MODEL), lambda i: (jnp.maximum(i - n_p, 0), 0))],
        out_specs=pl.BlockSpec((TILE, D_MODEL), lambda i: (i, 0)),
        out_shape=jax.ShapeDtypeStruct((ROWS, D_MODEL), F32),
        compiler_params=_params(("arbitrary",)),
        name="token_rows",
    )(x_prompt, meta, x_sample)


def _inproj_kernel(*refs):
    x_ref, g_ref, w_ref, perm_ref = refs[:4]
    pv_ref, q_ref, k_ref, v_ref, kp_ref, qt_ref, vt_ref = refs[-7:]
    hn = _rms(x_ref[...], g_ref[...]).astype(BF16)
    pv_ref[...] = jnp.dot(hn, w_ref[:, 0:512], preferred_element_type=F32)
    q = jnp.dot(hn, w_ref[:, 512:1024], preferred_element_type=F32) * (SB_HEAD_DIM ** -0.5)
    k = jnp.dot(hn, w_ref[:, 1024:1536], preferred_element_type=F32)
    v = jnp.dot(hn, w_ref[:, 1536:2048], preferred_element_type=F32)
    q_ref[...] = q.astype(BF16)
    k_ref[...] = k
    v_ref[...] = v
    perm = perm_ref[...]
    for half in range(TM // TILE):
        sl = slice(half * TILE, (half + 1) * TILE)
        kp_ref[sl, :] = jnp.dot(perm, k[sl].astype(BF16), preferred_element_type=F32).astype(BF16)
        vp = jnp.dot(perm, v[sl].astype(BF16), preferred_element_type=F32)
        vt_ref[half] = vp.T.astype(BF16)
        qt_ref[half] = (q[sl] * 0.5).T.astype(BF16)


def _inproj(h, gain, w, perm, layer, kv_bufs):
    row = lambda i: (i, 0)
    fixed = lambda i: (0, 0)
    tiles = lambda i: (i, 0, 0)
    lrow = lambda i: (layer, i, 0)
    per = TM // TILE
    kv_block = pl.BlockSpec((None, TM, 512), lrow)
    kv_shape = jax.ShapeDtypeStruct((DEPTH, ROWS, 512), F32)
    in_specs = [pl.BlockSpec((TM, D_MODEL), row),
                pl.BlockSpec((1, D_MODEL), fixed),
                pl.BlockSpec((D_MODEL, IN_WIDTH), fixed),
                pl.BlockSpec((TILE, TILE), fixed),
                pl.BlockSpec(memory_space=pl.ANY), pl.BlockSpec(memory_space=pl.ANY)]
    args = [h, gain, w, perm, *kv_bufs]
    aliases = {4: 2, 5: 3}
    return pl.pallas_call(
        _inproj_kernel,
        grid=(ROWS // TM,),
        in_specs=in_specs,
        out_specs=[pl.BlockSpec((TM, 512), row), pl.BlockSpec((TM, 512), row),
                   kv_block, kv_block,
                   pl.BlockSpec((TM, 512), row)]
        + [pl.BlockSpec((per, SB_WIDTH, TILE), tiles)] * 2,
        out_shape=[jax.ShapeDtypeStruct((ROWS, 512), F32),
                   jax.ShapeDtypeStruct((ROWS, 512), BF16),
                   kv_shape, kv_shape,
                   jax.ShapeDtypeStruct((ROWS, 512), BF16),
                   jax.ShapeDtypeStruct((ROWS // TILE, SB_WIDTH, TILE), BF16),
                   jax.ShapeDtypeStruct((ROWS // TILE, SB_WIDTH, TILE), BF16)],
        input_output_aliases=aliases,
        compiler_params=_params(("arbitrary",)),
        name="inproj",
    )(*args)


def _key_permutation():
    r = jnp.arange(TILE)
    src = (r % SUBLANES) * SEG + r // SUBLANES
    return (src[:, None] == jnp.arange(TILE)[None, :]).astype(BF16)


def _sb_half(zh, carry, lane0, masked):
    om = 0.5 - 0.5 * jnp.tanh(zh)
    if masked:
        r = lax.broadcasted_iota(I32, (TILE, LANES), 0)
        c = lax.broadcasted_iota(I32, (TILE, LANES), 1) + lane0
        key = ((r & (SUBLANES - 1)) * SEG) + (r >> 3)
        om = jnp.where(key < c, om, 1.0)
    run = jnp.ones((SUBLANES, LANES), F32)
    parts = [None] * SEG
    for a in reversed(range(SEG)):
        nxt = run * om[a * SUBLANES:(a + 1) * SUBLANES]
        parts[a] = run - nxt
        run = nxt
    sub = lax.broadcasted_iota(I32, (SUBLANES, LANES), 0)
    y = run
    for k in (1, 2, 4):
        y = y * jnp.where(sub + k < SUBLANES, pltpu.roll(y, SUBLANES - k, 0), 1.0)
    off = carry * jnp.where(sub + 1 < SUBLANES, pltpu.roll(y, SUBLANES - 1, 0), 1.0)
    new_carry = carry * jnp.broadcast_to(y[0:1, :], (SUBLANES, LANES))
    a_t = jnp.concatenate([p * off for p in parts], axis=0).astype(BF16)
    return a_t, new_carry


def _sb_tile(z, carry, masked):
    outs = []
    carries = []
    for lh in range(TILE // LANES):
        ls = slice(lh * LANES, (lh + 1) * LANES)
        a, c = _sb_half(z[:, ls], carry[:, ls], lh * LANES, masked)
        outs.append(a)
        carries.append(c)
    return jnp.concatenate(outs, axis=1), jnp.concatenate(carries, axis=1)


def _attn_prompt_kernel(nt_ref, k_ref, qt_ref, vt_ref, o_ref, zbuf, abuf):
    half = SB_HEAD_DIM
    row = lax.broadcasted_iota(I32, (LANES, TILE), 0)

    def k_tile(j):
        return k_ref[pl.ds(pl.multiple_of(j * TILE, TILE), TILE), :]

    def q_tile(i, _):
        qt = qt_ref[i].astype(F32)
        qh = (jnp.where(row < half, qt, 0.0).astype(BF16),
              jnp.where(row >= half, qt, 0.0).astype(BF16))
        ones = jnp.ones((SUBLANES, TILE), F32)

        kd = k_tile(i)
        carries = []
        for h in range(2):
            z = jnp.dot(kd, qh[h], preferred_element_type=F32)
            a, c = _sb_tile(z, ones, True)
            abuf[0, h] = a
            abuf[1, h] = jnp.zeros((TILE, TILE), BF16)
            carries.append(c)
        k0 = k_tile(jnp.maximum(i - 1, 0))
        k1 = k_tile(jnp.maximum(i - 2, 0))
        for h in range(2):
            zbuf[0, h] = jnp.dot(k0, qh[h], preferred_element_type=F32)
            zbuf[1, h] = jnp.dot(k1, qh[h], preferred_element_type=F32)

        def weighted_values(acc, v0, v1):
            out = []
            for h in range(2):
                rows = slice(h * half, (h + 1) * half)
                out.append(acc[h]
                           + jnp.dot(v0[rows], abuf[0, h], preferred_element_type=F32)
                           + jnp.dot(v1[rows], abuf[1, h], preferred_element_type=F32))
            return out

        def trip(st):
            t, _, c0, c1, acc0, acc1, p0, p1 = st
            j0 = i - 1 - 2 * t
            j1 = jnp.maximum(j0 - 1, 0)
            acc = weighted_values((acc0, acc1), vt_ref[p0], vt_ref[p1])
            cs = [c0, c1]
            for h in range(2):
                for s in range(2):
                    a, cs[h] = _sb_tile(zbuf[s, h], cs[h], False)
                    abuf[s, h] = a
            n0 = k_tile(jnp.maximum(j0 - 2, 0))
            n1 = k_tile(jnp.maximum(j0 - 3, 0))
            for h in range(2):
                zbuf[0, h] = jnp.dot(n0, qh[h], preferred_element_type=F32)
                zbuf[1, h] = jnp.dot(n1, qh[h], preferred_element_type=F32)
            alive = jnp.maximum(jnp.max(cs[0]), jnp.max(cs[1])) > 0.0
            return t + 1, alive, cs[0], cs[1], acc[0], acc[1], j0, j1

        zeros = jnp.zeros((half, TILE), F32)
        trips = (i + 1) // 2
        st = lax.while_loop(lambda s: (s[0] < trips) & s[1], trip,
                            (jnp.int32(0), jnp.bool_(True), carries[0], carries[1],
                             zeros, zeros, i, i))
        keep1 = (i == 0) | ((i & 1) == 0) | (st[0] < trips)
        v1 = jnp.where(keep1, vt_ref[st[7]].astype(F32), 0.0).astype(BF16)
        acc = weighted_values((st[4], st[5]), vt_ref[st[6]], v1)
        out_t = jnp.concatenate(acc, axis=0)
        o_ref[pl.ds(pl.multiple_of(i * TILE, TILE), TILE), :] = out_t.T.astype(BF16)
        return 0

    lax.fori_loop(0, nt_ref[0], q_tile, 0)


def _attn_prompt(kp, qt3, vt3, *, batch=BATCH, n_tiles=SEQ_TILES):
    rows = n_tiles * TILE
    return pl.pallas_call(
        _attn_prompt_kernel,
        grid=(batch, SB_HEADS // 2),
        in_specs=[pl.BlockSpec(memory_space=pltpu.SMEM),
                  pl.BlockSpec((rows, LANES), lambda p, g: (p, g)),
                  pl.BlockSpec((n_tiles, LANES, TILE), lambda p, g: (p, g, 0)),
                  pl.BlockSpec((n_tiles, LANES, TILE), lambda p, g: (p, g, 0))],
        out_specs=pl.BlockSpec((rows, LANES), lambda p, g: (p, g)),
        out_shape=jax.ShapeDtypeStruct((batch * rows, SB_WIDTH), BF16),
        scratch_shapes=[pltpu.VMEM((2, 2, TILE, TILE), F32),
                        pltpu.VMEM((2, 2, TILE, TILE), BF16)],
        compiler_params=_params(("arbitrary", "arbitrary")),
        name="attn_prompt",
    )(jnp.full((1,), n_tiles, I32), kp, qt3, vt3)


def _rev_excl_cumprod(om, scr):
    n = om.shape[0]
    sub = lax.broadcasted_iota(I32, (n, LANES), 0) & (SUBLANES - 1)
    scr[n:n + SUBLANES, :] = jnp.ones((SUBLANES, LANES), F32)
    y = om
    for k in (1, 2, 4):
        scr[0:n, :] = y
        y = y * jnp.where(sub + k < SUBLANES, scr[k:n + k, :], 1.0)
    scr[0:n, :] = y
    ex = jnp.where(sub + 1 < SUBLANES, scr[1:n + 1, :], 1.0)
    run = jnp.ones((SUBLANES, LANES), F32)
    out = [None] * (n // SUBLANES)
    for a in reversed(range(n // SUBLANES)):
        sl = slice(a * SUBLANES, (a + 1) * SUBLANES)
        out[a] = ex[sl] * run
        run = run * jnp.broadcast_to(y[a * SUBLANES:a * SUBLANES + 1, :], (SUBLANES, LANES))
    return jnp.concatenate(out, axis=0)


def _attn_sample_kernel(ck_ref, cv_ref, kn_ref, vn_ref, qbd_ref, o_ref, scr):
    qbd = qbd_ref[0]
    pad = jnp.zeros((LANES - DEC_SEQ, SB_WIDTH), F32)
    kn = jnp.concatenate([kn_ref[...], pad], axis=0).astype(BF16)
    vn = jnp.concatenate([vn_ref[...], pad], axis=0).astype(BF16)
    zc = jnp.dot(ck_ref[0].astype(BF16), qbd, preferred_element_type=F32)
    zn = jnp.dot(kn, qbd, preferred_element_type=F32)
    z = jnp.concatenate([zc, zn], axis=0)
    n = PAST_LEN + LANES
    r = lax.broadcasted_iota(I32, (n, LANES), 0)
    c = lax.broadcasted_iota(I32, (n, LANES), 1)
    valid = (r < PAST_LEN) | ((r - PAST_LEN) < (c & (DEC_SEQ - 1)))
    e = jnp.exp(z)
    om = 1.0 / (1.0 + e)
    beta = jnp.where(valid, 1.0 - om, 0.0)
    om = jnp.where(valid, om, 1.0)
    a_t = beta * _rev_excl_cumprod(om, scr)
    ac = a_t[0:PAST_LEN].T.astype(BF16)
    an = a_t[PAST_LEN:n].T.astype(BF16)
    p = jnp.dot(ac, cv_ref[0].astype(BF16), preferred_element_type=F32)
    p = p + jnp.dot(an, vn, preferred_element_type=F32)
    pr = lax.broadcasted_iota(I32, (LANES, SB_WIDTH), 0)
    pc = lax.broadcasted_iota(I32, (LANES, SB_WIDTH), 1)
    p = jnp.where((pr >> 4) == (pc >> 6), p, 0.0)
    out = p[0:DEC_SEQ]
    for h in range(1, SB_HEADS):
        out = out + p[h * DEC_SEQ:(h + 1) * DEC_SEQ]
    o_ref[...] = out.astype(BF16)


def _attn_sample(ck, cv, k, v, qbd, layer):
    new = lambda s: (layer, ROWS_P // DEC_SEQ + s, 0)
    per = lambda s: (s, 0, 0)
    return pl.pallas_call(
        _attn_sample_kernel,
        grid=(DEC_BATCH,),
        in_specs=[pl.BlockSpec((1, PAST_LEN, SB_WIDTH), per),
                  pl.BlockSpec((1, PAST_LEN, SB_WIDTH), per),
                  pl.BlockSpec((None, DEC_SEQ, SB_WIDTH), new),
                  pl.BlockSpec((None, DEC_SEQ, SB_WIDTH), new),
                  pl.BlockSpec((1, SB_WIDTH, LANES), per)],
        out_specs=pl.BlockSpec((DEC_SEQ, SB_WIDTH), lambda s: (s, 0)),
        out_shape=jax.ShapeDtypeStruct((ROWS_S, SB_WIDTH), BF16),
        scratch_shapes=[pltpu.VMEM((PAST_LEN + LANES + SUBLANES, LANES), F32)],
        compiler_params=_params(("arbitrary",)),
        name="attn_sample",
    )(ck, cv, k, v, qbd)


def _pool_kernel(*refs, tm, use_halo):
    if use_halo:
        pv_ref, halo_ref, w_ref, sc_ref, o_ref, s1, s2, s4, s8 = refs
    else:
        pv_ref, w_ref, sc_ref, o_ref, s1, s2, s4, s8 = refs
    i = pl.program_id(0)
    off = SUBLANES
    n = HALO + tm
    x = pv_ref[...]
    zero8 = jnp.zeros((off, POOL_WIDTH), F32)
    if use_halo:
        first = (i % (SEQ_PAD // tm)) == 0
        halo = jnp.where(first, 0.0, halo_ref[...])
    else:
        halo = jnp.zeros((HALO, POOL_WIDTH), F32)
    s1[0:off, :] = zero8
    s1[off:off + HALO, :] = halo
    s1[off + HALO:off + n, :] = x
    g = POOL_GROUP_DIM
    t2 = s1[off:off + n, :] + s1[off - 1:off - 1 + n, :]
    s2[0:off, :] = zero8[:, g:]
    s2[off:off + n, :] = t2[:, g:]
    t4 = t2[:, g:] + s2[off - 2:off - 2 + n, :]
    s4[0:off, :] = zero8[:, 2 * g:]
    s4[off:off + n, :] = t4[:, g:]
    t8 = t4[:, g:] + s4[off - 4:off - 4 + n, :]
    s8[0:off, :] = zero8[:, 3 * g:]
    s8[off:off + n, :] = t8[:, g:]
    t16 = t8[:, g:] + s8[off - 8:off - 8 + n, :]
    sums = (t2[HALO:, 0:g], t4[HALO:, 0:g], t8[HALO:, 0:g], t16[HALO:, :])
    if use_halo:
        idx = (i % (SEQ_PAD // tm)) * tm - FRONT + lax.broadcasted_iota(I32, (tm, g), 0)
    for gi, win in enumerate(POOL_WINDOWS):
        if use_halo:
            cnt = jnp.clip(idx + 1, 1, win).astype(F32)
        else:
            cnt = jnp.full((tm, g), float(win), F32)
        d = sums[gi] / cnt - x[:, gi * g:(gi + 1) * g]
        y = jnp.dot(d.astype(BF16), w_ref[gi], preferred_element_type=F32)
        o_ref[:, gi * g:(gi + 1) * g] = (y * sc_ref[:, gi * g:(gi + 1) * g]).astype(BF16)


def _pool(pv, w, scale, *, rows, tm, use_halo):
    g = POOL_GROUP_DIM
    in_specs = [pl.BlockSpec((tm, POOL_WIDTH), lambda i: (i, 0))]
    args = [pv]
    if use_halo:
        in_specs.append(pl.BlockSpec((HALO, POOL_WIDTH),
                                     lambda i: (jnp.maximum(i * (tm // HALO) - 1, 0), 0)))
        args.append(pv)
    in_specs += [pl.BlockSpec((4, g, g), lambda i: (0, 0, 0)),
                 pl.BlockSpec((1, POOL_WIDTH), lambda i: (0, 0))]
    args += [w, scale]
    n = SUBLANES + HALO + tm
    return pl.pallas_call(
        functools.partial(_pool_kernel, tm=tm, use_halo=use_halo),
        grid=(rows // tm,),
        in_specs=in_specs,
        out_specs=pl.BlockSpec((tm, POOL_WIDTH), lambda i: (i, 0)),
        out_shape=jax.ShapeDtypeStruct((rows, POOL_WIDTH), BF16),
        scratch_shapes=[pltpu.VMEM((n, 4 * g), F32), pltpu.VMEM((n, 3 * g), F32),
                        pltpu.VMEM((n, 2 * g), F32), pltpu.VMEM((n, g), F32)],
        compiler_params=_params(("arbitrary",)),
        name="pool_prompt" if use_halo else "pool_sample",
    )(*args)


def _mix_kernel(*refs, route):
    if route:
        (h_ref, pp_ref, ps_ref, sp_ref, ss_ref, w_ref, g_ref, r_ref,
         hmid_ref, hn_ref, rt_ref) = refs
    else:
        h_ref, pp_ref, ps_ref, sp_ref, ss_ref, w_ref, g_ref, hmid_ref, hn_ref = refs
    prompt = pl.program_id(0) < ROWS_P // TM
    pool = jnp.where(prompt, pp_ref[...], ps_ref[...])
    sb = jnp.where(prompt, sp_ref[...], ss_ref[...])
    hmid = (h_ref[...]
            + jnp.dot(pool, w_ref[0:POOL_WIDTH, :], preferred_element_type=F32)
            + jnp.dot(sb, w_ref[POOL_WIDTH:, :], preferred_element_type=F32))
    hmid_ref[...] = hmid
    hn = _rms(hmid, g_ref[...])
    if not route:
        hn_ref[...] = hn.astype(BF16)
    else:
        bits = pltpu.bitcast(hn.astype(BF16).astype(F32), jnp.uint32)
        half = D_MODEL // 2
        hn_ref[...] = (bits[:, :half] >> 16) | (bits[:, half:] & jnp.uint32(0xFFFF0000))
        tm = hn.shape[0]
        h_hi = hn.astype(BF16)
        h_lo = (hn - h_hi.astype(F32)).astype(BF16)
        r = r_ref[...]
        r_hi = r.astype(BF16)
        r_lo = (r - r_hi.astype(F32)).astype(BF16)
        logits = (jnp.dot(h_hi, r_hi, preferred_element_type=F32)
                  + jnp.dot(h_hi, r_lo, preferred_element_type=F32)
                  + jnp.dot(h_lo, r_hi, preferred_element_type=F32))
        lane = lax.broadcasted_iota(I32, (tm, LANES), 1)
        neg = jnp.float32(-jnp.inf)
        logits = jnp.where(lane < N_EXPERTS, logits, neg)
        m1 = jnp.max(logits, axis=-1, keepdims=True)
        i1 = jnp.min(jnp.where(logits == m1, lane, LANES), axis=-1, keepdims=True)
        rest = jnp.where(lane == i1, neg, logits)
        m2 = jnp.max(rest, axis=-1, keepdims=True)
        i2 = jnp.min(jnp.where(rest == m2, lane, LANES), axis=-1, keepdims=True)
        t = jnp.exp(m2 - m1)
        g1 = 1.0 / (1.0 + t)
        g2 = t / (1.0 + t)
        rt_ref[...] = (jnp.where(lane == 0, i1.astype(F32), 0.0)
                       + jnp.where(lane == 1, i2.astype(F32), 0.0)
                       + jnp.where(lane == 2, g1, 0.0) + jnp.where(lane == 3, g2, 0.0))


def _mix(h, pool_p, pool_s, sb_p, sb_s, w, gain, router=None):
    row = lambda i: (i, 0)
    fixed = lambda i: (0, 0)
    prow = lambda i: (jnp.minimum(i, ROWS_P // TM - 1), 0)
    route = router is not None
    in_specs = [pl.BlockSpec((TM, D_MODEL), row),
                pl.BlockSpec((TM, POOL_WIDTH), prow),
                pl.BlockSpec((TM, POOL_WIDTH), fixed),
                pl.BlockSpec((TM, SB_WIDTH), prow),
                pl.BlockSpec((TM, SB_WIDTH), fixed),
                pl.BlockSpec((D_MODEL, D_MODEL), fixed),
                pl.BlockSpec((1, D_MODEL), fixed)]
    hn_cols = D_MODEL // 2 if route else D_MODEL
    out_specs = [pl.BlockSpec((TM, D_MODEL), row), pl.BlockSpec((TM, hn_cols), row)]
    out_shape = [jax.ShapeDtypeStruct((ROWS, D_MODEL), F32),
                 jax.ShapeDtypeStruct((ROWS, hn_cols), jnp.uint32 if route else BF16)]
    args = [h, pool_p, pool_s, sb_p, sb_s, w, gain]
    if route:
        in_specs.append(pl.BlockSpec((D_MODEL, LANES), fixed))
        out_specs.append(pl.BlockSpec((TM, LANES), row))
        out_shape.append(jax.ShapeDtypeStruct((ROWS, LANES), F32))
        args.append(router)
    return pl.pallas_call(
        functools.partial(_mix_kernel, route=route),
        grid=(ROWS // TM,),
        in_specs=in_specs, out_specs=out_specs, out_shape=out_shape,
        compiler_params=_params(("arbitrary",)),
        name="mix_route" if route else "mix",
    )(*args)


def _silu_mul(g, u):
    return g * (1.0 / (1.0 + jnp.exp(-g))) * u


FF_CHUNK = 256


def _ffn_kernel(hmid_ref, hn_ref, wg_ref, wu_ref, wd_ref, o_ref, act):
    hn = hn_ref[...]
    for c in range(D_FF // FF_CHUNK):
        sl = slice(c * FF_CHUNK, (c + 1) * FF_CHUNK)
        g = jnp.dot(hn, wg_ref[:, sl], preferred_element_type=F32)
        u = jnp.dot(hn, wu_ref[:, sl], preferred_element_type=F32)
        act[:, sl] = _silu_mul(g, u).astype(BF16)
    o_ref[...] = hmid_ref[...] + jnp.dot(act[...], wd_ref[...], preferred_element_type=F32)


def _ffn(hmid, hn, wg, wu, wd):
    row = lambda i: (i, 0)
    fixed = lambda i: (0, 0)
    once = pl.Buffered(1)
    return pl.pallas_call(
        _ffn_kernel,
        grid=(ROWS // TM,),
        in_specs=[pl.BlockSpec((TM, D_MODEL), row),
                  pl.BlockSpec((TM, D_MODEL), row),
                  pl.BlockSpec((D_MODEL, D_FF), fixed, pipeline_mode=once),
                  pl.BlockSpec((D_MODEL, D_FF), fixed, pipeline_mode=once),
                  pl.BlockSpec((D_FF, D_MODEL), fixed, pipeline_mode=once)],
        out_specs=pl.BlockSpec((TM, D_MODEL), row),
        out_shape=jax.ShapeDtypeStruct((ROWS, D_MODEL), F32),
        scratch_shapes=[pltpu.VMEM((TM, D_FF), BF16)],
        compiler_params=_params(("arbitrary",)),
        name="ffn",
    )(hmid, hn, wg, wu, wd)


def _route_plan(route):
    e_flat = jnp.concatenate([route[:, 0], route[:, 1]]).astype(I32)
    order = jnp.argsort(e_flat).astype(I32)
    ex = jnp.arange(N_EXPERTS, dtype=I32)
    counts = jnp.sum((e_flat[:, None] == ex[None, :]).astype(I32), axis=0)
    gstart = jnp.cumsum(counts) - counts
    padded = ((counts + TM_E - 1) // TM_E) * TM_E
    pend = jnp.cumsum(padded)
    pstart = pend - padded
    q = jnp.arange(P_MAX, dtype=I32)
    e_q = jnp.sum((q[:, None] >= pend[None, :]).astype(I32), axis=1)
    e_c = jnp.minimum(e_q, N_EXPERTS - 1)
    within = q - pstart[e_c]
    valid = (e_q < N_EXPERTS) & (within < counts[e_c])
    pair = order[jnp.clip(gstart[e_c] + within, 0, PAIRS - 1)]
    src = jnp.where(valid, jnp.where(pair >= ROWS, pair - ROWS, pair), 0)
    spare = PAIRS + ((q // TM_E) % 2) * TM_E + (q % TM_E)
    dst = jnp.where(valid, pair, spare)
    tile_expert = e_c[::TM_E]
    n_used = (pend[-1] // TM_E).reshape(1)
    return (src.reshape(NT_E, 1, TM_E), dst.reshape(NT_E, 1, TM_E), tile_expert, n_used)


DMA_UNROLL = 8


def _row_dmas(per, make):
    def issue(r8, _):
        for u in range(DMA_UNROLL):
            make(r8 * DMA_UNROLL + u).start(priority=u % 2)
        return 0

    lax.fori_loop(0, per // DMA_UNROLL, issue, 0)


def _gather_kernel(nu_ref, idx_ref, src_ref, out_ref, sem):
    per = out_ref.shape[0]
    used = pl.program_id(0) < nu_ref[0]

    @pl.when(used)
    def _():
        _row_dmas(per, lambda r: pltpu.make_async_copy(
            src_ref.at[pl.ds(idx_ref[0, 0, r], 1)], out_ref.at[pl.ds(r, 1)], sem))
        pltpu.make_async_copy(src_ref.at[pl.ds(0, per)], out_ref, sem).wait()

    @pl.when(jnp.logical_not(used))
    def _():
        out_ref[...] = jnp.zeros(out_ref.shape, out_ref.dtype)


def _gather_rows(n_used, idx, src):
    n_steps, _, per = idx.shape
    grid_spec = pltpu.PrefetchScalarGridSpec(
        num_scalar_prefetch=1,
        grid=(n_steps,),
        in_specs=[pl.BlockSpec((1, 1, per), lambda s, nu: (s, 0, 0), memory_space=pltpu.SMEM),
                  pl.BlockSpec(memory_space=pl.ANY)],
        out_specs=pl.BlockSpec((per, src.shape[1]), lambda s, nu: (s, 0)),
        scratch_shapes=[pltpu.SemaphoreType.DMA(())])
    return pl.pallas_call(
        _gather_kernel,
        grid_spec=grid_spec,
        out_shape=jax.ShapeDtypeStruct((n_steps * per, src.shape[1]), src.dtype),
        compiler_params=_params(("arbitrary",)),
        name="moe_gather",
    )(n_used, idx, src)


def _scatter_kernel(nu_ref, idx_ref, in_ref, dst_ref, zeros, sem):
    per = in_ref.shape[0]

    @pl.when(pl.program_id(0) == 0)
    def _():
        zeros[...] = jnp.zeros(zeros.shape, zeros.dtype)
        spare = [pltpu.make_async_copy(zeros, dst_ref.at[pl.ds(PAIRS + k * per, per)], sem)
                 for k in range((dst_ref.shape[0] - PAIRS) // per)]
        for cp in spare:
            cp.start()
        for cp in spare:
            cp.wait()

    @pl.when(pl.program_id(0) < nu_ref[0])
    def _():
        _row_dmas(per, lambda r: pltpu.make_async_copy(
            in_ref.at[pl.ds(r, 1)], dst_ref.at[pl.ds(idx_ref[0, 0, r], 1)], sem))
        pltpu.make_async_copy(in_ref, dst_ref.at[pl.ds(0, per)], sem).wait()


def _scatter_rows(n_used, idx, src, out_rows):
    n_steps, _, per = idx.shape
    held = lambda s, nu: (jnp.minimum(s, nu[0] - 1), 0)
    grid_spec = pltpu.PrefetchScalarGridSpec(
        num_scalar_prefetch=1,
        grid=(n_steps,),
        in_specs=[pl.BlockSpec((1, 1, per), lambda s, nu: (s, 0, 0), memory_space=pltpu.SMEM),
                  pl.BlockSpec((per, src.shape[1]), held)],
        out_specs=pl.BlockSpec(memory_space=pl.ANY),
        scratch_shapes=[pltpu.VMEM((per, src.shape[1]), src.dtype), pltpu.SemaphoreType.DMA(())])
    return pl.pallas_call(
        _scatter_kernel,
        grid_spec=grid_spec,
        out_shape=jax.ShapeDtypeStruct((out_rows, src.shape[1]), src.dtype),
        compiler_params=_params(("arbitrary",)),
        name="moe_scatter",
    )(n_used, idx, src)


def _experts_kernel(te_ref, nu_ref, x_ref, wg_ref, wu_ref, wd_ref, y_ref, act):
    t = pl.program_id(0)
    c = pl.program_id(1)

    @pl.when(t < nu_ref[0])
    def _():
        w = x_ref[...]
        lo = pltpu.bitcast(w << 16, F32).astype(BF16)
        hi = pltpu.bitcast(w & jnp.uint32(0xFFFF0000), F32).astype(BF16)
        x = jnp.concatenate([lo, hi], axis=1)
        for s in range(EX_CHUNK // FF_CHUNK):
            sl = slice(s * FF_CHUNK, (s + 1) * FF_CHUNK)
            g = jnp.dot(x, wg_ref[0, :, sl], preferred_element_type=F32)
            u = jnp.dot(x, wu_ref[0, :, sl], preferred_element_type=F32)
            act[:, sl] = _silu_mul(g, u).astype(BF16)
        y = jnp.dot(act[...], wd_ref[0], preferred_element_type=F32)

        @pl.when(c == 0)
        def _():
            y_ref[...] = y

        @pl.when(c > 0)
        def _():
            y_ref[...] += y

    @pl.when((t >= nu_ref[0]) & (c == 0))
    def _():
        y_ref[...] = jnp.zeros(y_ref.shape, y_ref.dtype)


def _experts(tile_expert, n_used, xs, wg, wu, wd):
    n_chunks = D_EXPERT // EX_CHUNK

    def held(t, c, nu):
        live = t < nu[0]
        return jnp.where(live, t, nu[0] - 1), jnp.where(live, c, n_chunks - 1)

    def x_map(t, c, te, nu):
        return held(t, c, nu)[0], 0

    def w_in_map(t, c, te, nu):
        tt, cc = held(t, c, nu)
        return te[tt], 0, cc

    def w_out_map(t, c, te, nu):
        tt, cc = held(t, c, nu)
        return te[tt], cc, 0

    grid_spec = pltpu.PrefetchScalarGridSpec(
        num_scalar_prefetch=2,
        grid=(NT_E, n_chunks),
        in_specs=[pl.BlockSpec((TM_E, D_MODEL // 2), x_map),
                  pl.BlockSpec((1, D_MODEL, EX_CHUNK), w_in_map),
                  pl.BlockSpec((1, D_MODEL, EX_CHUNK), w_in_map),
                  pl.BlockSpec((1, EX_CHUNK, D_MODEL), w_out_map)],
        out_specs=pl.BlockSpec((TM_E, D_MODEL), lambda t, c, te, nu: (t, 0)),
        scratch_shapes=[pltpu.VMEM((TM_E, EX_CHUNK), BF16)])
    return pl.pallas_call(
        _experts_kernel,
        grid_spec=grid_spec,
        out_shape=jax.ShapeDtypeStruct((P_MAX, D_MODEL), F32),
        compiler_params=_params(("arbitrary", "arbitrary")),
        name="experts",
    )(tile_expert, n_used, xs, wg, wu, wd)


def _combine_kernel(hmid_ref, a_ref, b_ref, rt_ref, fg_ref, yp_ref, ys_ref):
    i = pl.program_id(0)
    lane = lax.broadcasted_iota(I32, rt_ref.shape, 1)
    rt = rt_ref[...]
    g1 = jnp.sum(jnp.where(lane == 2, rt, 0.0), axis=-1, keepdims=True)
    g2 = jnp.sum(jnp.where(lane == 3, rt, 0.0), axis=-1, keepdims=True)
    y = _rms(hmid_ref[...] + g1 * a_ref[...] + g2 * b_ref[...], fg_ref[...])
    prompt = i < ROWS_P // TILE

    @pl.when(prompt & (i % SEQ_TILES != 0))
    def _():
        yp_ref[0] = y

    @pl.when(jnp.logical_not(prompt))
    def _():
        ys_ref[...] = y


def _combine(hmid, slots, route, fgain):
    n_p = ROWS_P // TILE

    def yp_map(i):
        ip = jnp.minimum(i, n_p - 1)
        return (ip // SEQ_TILES, jnp.maximum(ip % SEQ_TILES - 1, 0), 0)

    return pl.pallas_call(
        _combine_kernel,
        grid=(ROWS // TILE,),
        in_specs=[pl.BlockSpec((TILE, D_MODEL), lambda i: (i, 0)),
                  pl.BlockSpec((TILE, D_MODEL), lambda i: (i, 0)),
                  pl.BlockSpec((TILE, D_MODEL), lambda i: (i + ROWS // TILE, 0)),
                  pl.BlockSpec((TILE, LANES), lambda i: (i, 0)),
                  pl.BlockSpec((1, D_MODEL), lambda i: (0, 0))],
        out_specs=[pl.BlockSpec((1, TILE, D_MODEL), yp_map),
                   pl.BlockSpec((TILE, D_MODEL), lambda i: (jnp.maximum(i - n_p, 0), 0))],
        out_shape=[jax.ShapeDtypeStruct((BATCH, SEQ, D_MODEL), F32),
                   jax.ShapeDtypeStruct((ROWS_S, D_MODEL), F32)],
        compiler_params=_params(("arbitrary",)),
        name="combine",
    )(hmid, slots, slots, route, fgain)


def _moe(hmid, hn, route, wg, wu, wd, fgain):
    src, dst, tile_expert, n_used = _route_plan(route)
    xs = _gather_rows(n_used, src, hn)
    ys = _experts(tile_expert, n_used, xs, wg, wu, wd)
    slots = _scatter_rows(n_used, dst, ys, SLOT_ROWS)
    return _combine(hmid, slots, route, fgain)


def _sample_qbd(q):
    qs = q[ROWS_P:].reshape(DEC_BATCH, DEC_SEQ, SB_HEADS, SB_HEAD_DIM)
    eye = jnp.eye(SB_HEADS, dtype=q.dtype)
    qbd = jnp.einsum('sthd,hg->shdgt', qs, eye)
    return qbd.reshape(DEC_BATCH, SB_WIDTH, SB_HEADS * DEC_SEQ)


def kernel(x_prompt, x_sample, cache_k, cache_v, state_pool, meta_tokens, norm_mix, w_in, pool_w, pool_scale, w_out, norm_ffn, ffn_w_gate, ffn_w_up, ffn_w_down, moe_router, moe_w_gate, moe_w_up, moe_w_down, final_norm):
    h = _token_rows(x_prompt, meta_tokens, x_sample.reshape(ROWS_S, D_MODEL))
    perm = _key_permutation()

    outs = {name: [] for name in ("pp", "sp")}
    y_prompt = y_sample = None
    kv_bufs = (jnp.zeros((DEPTH, ROWS, 512), F32), jnp.zeros((DEPTH, ROWS, 512), F32))
    for l in range(DEPTH):
        pv, q, k, v, kp, qt3, vt3 = _inproj(h, norm_mix[l][None], w_in[l].astype(BF16), perm,
                                            l, kv_bufs)
        kv_bufs = (k, v)

        sb_p = _attn_prompt(kp, qt3, vt3)
        sb_s = _attn_sample(cache_k[l].reshape(DEC_BATCH, PAST_LEN, SB_WIDTH),
                            cache_v[l].reshape(DEC_BATCH, PAST_LEN, SB_WIDTH),
                            k, v, _sample_qbd(q), l)

        pw = pool_w[l].astype(BF16)
        ps = pool_scale[l][None]
        pool_p = _pool(pv, pw, ps, rows=ROWS_P, tm=TILE, use_halo=True)
        pv_s = pv[ROWS_P:].reshape(DEC_BATCH, DEC_SEQ, POOL_WIDTH)
        pbuf = jnp.concatenate([jnp.zeros((DEC_BATCH, 1, POOL_WIDTH), F32), state_pool[l], pv_s],
                               axis=1).reshape(DEC_BATCH * 2 * DEC_SEQ, POOL_WIDTH)
        pool_s = _pool(pbuf, pw, ps, rows=DEC_BATCH * 2 * DEC_SEQ, tm=TM, use_halo=False)
        pool_s = pool_s.reshape(DEC_BATCH, 2 * DEC_SEQ, POOL_WIDTH)[:, DEC_SEQ:].reshape(ROWS_S, POOL_WIDTH)

        wo = w_out[l].astype(BF16)
        j = l // 2
        if l % 2 == 0:
            hmid, hn = _mix(h, pool_p, pool_s, sb_p, sb_s, wo, norm_ffn[l][None])
            h = _ffn(hmid, hn, ffn_w_gate[j].astype(BF16), ffn_w_up[j].astype(BF16),
                     ffn_w_down[j].astype(BF16))
        else:
            router = jnp.pad(moe_router[j], ((0, 0), (0, LANES - N_EXPERTS)))
            hmid, hn, route = _mix(h, pool_p, pool_s, sb_p, sb_s, wo, norm_ffn[l][None], router)
            y_prompt, y_sample = _moe(hmid, hn, route, moe_w_gate[j].astype(BF16),
                                      moe_w_up[j].astype(BF16), moe_w_down[j].astype(BF16),
                                      final_norm[None])

        outs["pp"].append(pv[:ROWS_P].reshape(BATCH, SEQ_PAD, POOL_WIDTH)[:, -POOL_STATE:])
        outs["sp"].append(pv_s[:, -POOL_STATE:])

    def prompt_rows(buf):
        rows = buf[:, :ROWS_P].reshape(DEPTH, BATCH, SEQ_PAD, SB_HEADS, SB_HEAD_DIM)
        return rows[:, :, FRONT:]

    def sample_rows(buf):
        return buf[:, ROWS_P:].reshape(DEPTH, DEC_BATCH, DEC_SEQ, SB_HEADS, SB_HEAD_DIM)

    k, v = kv_bufs
    y_sample = y_sample.reshape(DEC_BATCH, DEC_SEQ, D_MODEL)
    return (y_prompt, y_sample, prompt_rows(k), prompt_rows(v), jnp.stack(outs["pp"]),
            sample_rows(k), sample_rows(v), jnp.stack(outs["sp"]))
```

```python
import functools

import jax
import jax.numpy as jnp
from jax import lax
from jax.experimental import pallas as pl
from jax.experimental.pallas import tpu as pltpu

F32 = jnp.float32
BF16 = jnp.bfloat16
I32 = jnp.int32

D_MODEL = 1024
BATCH = 2
SEQ = 8192
DEPTH = 2
DEC_BATCH = 32
DEC_SEQ = 16
PAST_LEN = 1024
N_META = 16
POOL_WIDTH = 512
POOL_WINDOWS = (2, 4, 8, 16)
POOL_GROUP_DIM = 128
POOL_STATE = 15
SB_HEADS = 8
SB_HEAD_DIM = 64
SB_WIDTH = 512
IN_WIDTH = 2048
D_FF = 2816
N_EXPERTS = 8
D_EXPERT = 3584
EPS = 1e-6

LANES = 128
SUBLANES = 8
TILE = 256
SEG = TILE // SUBLANES
FRONT = TILE - N_META
SEQ_PAD = FRONT + N_META + SEQ
SEQ_TILES = SEQ_PAD // TILE
ROWS_P = BATCH * SEQ_PAD
ROWS_S = DEC_BATCH * DEC_SEQ
ROWS = ROWS_P + ROWS_S
TM = 512
HALO = 16
VMEM_LIMIT = 56 * 1024 * 1024

TM_E = 512
PAIRS = 2 * ROWS
P_MAX = PAIRS + N_EXPERTS * TM_E
NT_E = P_MAX // TM_E
SLOT_ROWS = PAIRS + 2 * TM_E
EX_CHUNK = 1792


def _params(sem, vmem=VMEM_LIMIT):
    return pltpu.CompilerParams(dimension_semantics=sem, vmem_limit_bytes=vmem)


def _rms(x, gain):
    ms = jnp.mean(x * x, axis=-1, keepdims=True)
    return x * lax.rsqrt(ms + EPS) * gain


def _token_rows_kernel(xp_ref, meta_ref, xs_ref, o_ref):
    i = pl.program_id(0)
    t = i % SEQ_TILES
    prompt = i < ROWS_P // TILE

    @pl.when(prompt & (t == 0))
    def _():
        o_ref[...] = jnp.concatenate([jnp.zeros((FRONT, D_MODEL), F32), meta_ref[...]], axis=0)

    @pl.when(prompt & (t > 0))
    def _():
        o_ref[...] = xp_ref[0]

    @pl.when(jnp.logical_not(prompt))
    def _():
        o_ref[...] = xs_ref[...]


def _token_rows(x_prompt, meta, x_sample):
    n_p = ROWS_P // TILE

    def xp_map(i):
        ip = jnp.minimum(i, n_p - 1)
        return ip // SEQ_TILES, jnp.maximum(ip % SEQ_TILES - 1, 0), 0

    return pl.pallas_call(
        _token_rows_kernel,
        grid=(ROWS // TILE,),
        in_specs=[pl.BlockSpec((1, TILE, D_MODEL), xp_map),
                  pl.BlockSpec((N_META, D_MODEL), lambda i: (0, 0)),
                  pl.BlockSpec((TILE, D_MODEL), lambda i: (jnp.maximum(i - n_p, 0), 0))],
        out_specs=pl.BlockSpec((TILE, D_MODEL), lambda i: (i, 0)),
        out_shape=jax.ShapeDtypeStruct((ROWS, D_MODEL), F32),
        compiler_params=_params(("arbitrary",)),
        name="token_rows",
    )(x_prompt, meta, x_sample)


def _inproj_kernel(*refs):
    x_ref, g_ref, w_ref, perm_ref = refs[:4]
    pv_ref, q_ref, k_ref, v_ref, kp_ref, qt_ref, vt_ref = refs[-7:]
    hn = _rms(x_ref[...], g_ref[...]).astype(BF16)
    pv_ref[...] = jnp.dot(hn, w_ref[:, 0:512], preferred_element_type=F32)
    q = jnp.dot(hn, w_ref[:, 512:1024], preferred_element_type=F32) * (SB_HEAD_DIM ** -0.5)
    k = jnp.dot(hn, w_ref[:, 1024:1536], preferred_element_type=F32)
    v = jnp.dot(hn, w_ref[:, 1536:2048], preferred_element_type=F32)
    q_ref[...] = q.astype(BF16)
    k_ref[...] = k
    v_ref[...] = v
    perm = perm_ref[...]
    for half in range(TM // TILE):
        sl = slice(half * TILE, (half + 1) * TILE)
        kp_ref[sl, :] = jnp.dot(perm, k[sl].astype(BF16), preferred_element_type=F32).astype(BF16)
        vp = jnp.dot(perm, v[sl].astype(BF16), preferred_element_type=F32)
        vt_ref[half] = vp.T.astype(BF16)
        qt_ref[half] = (q[sl] * 0.5).T.astype(BF16)


def _inproj(h, gain, w, perm, layer, kv_bufs):
    row = lambda i: (i, 0)
    fixed = lambda i: (0, 0)
    tiles = lambda i: (i, 0, 0)
    lrow = lambda i: (layer, i, 0)
    per = TM // TILE
    kv_block = pl.BlockSpec((None, TM, 512), lrow)
    kv_shape = jax.ShapeDtypeStruct((DEPTH, ROWS, 512), F32)
    in_specs = [pl.BlockSpec((TM, D_MODEL), row),
                pl.BlockSpec((1, D_MODEL), fixed),
                pl.BlockSpec((D_MODEL, IN_WIDTH), fixed),
                pl.BlockSpec((TILE, TILE), fixed),
                pl.BlockSpec(memory_space=pl.ANY), pl.BlockSpec(memory_space=pl.ANY)]
    args = [h, gain, w, perm, *kv_bufs]
    aliases = {4: 2, 5: 3}
    return pl.pallas_call(
        _inproj_kernel,
        grid=(ROWS // TM,),
        in_specs=in_specs,
        out_specs=[pl.BlockSpec((TM, 512), row), pl.BlockSpec((TM, 512), row),
                   kv_block, kv_block,
                   pl.BlockSpec((TM, 512), row)]
        + [pl.BlockSpec((per, SB_WIDTH, TILE), tiles)] * 2,
        out_shape=[jax.ShapeDtypeStruct((ROWS, 512), F32),
                   jax.ShapeDtypeStruct((ROWS, 512), BF16),
                   kv_shape, kv_shape,
                   jax.ShapeDtypeStruct((ROWS, 512), BF16),
                   jax.ShapeDtypeStruct((ROWS // TILE, SB_WIDTH, TILE), BF16),
                   jax.ShapeDtypeStruct((ROWS // TILE, SB_WIDTH, TILE), BF16)],
        input_output_aliases=aliases,
        compiler_params=_params(("arbitrary",)),
        name="inproj",
    )(*args)


def _key_permutation():
    r = jnp.arange(TILE)
    src = (r % SUBLANES) * SEG + r // SUBLANES
    return (src[:, None] == jnp.arange(TILE)[None, :]).astype(BF16)


def _sb_half(zh, carry, lane0, masked):
    om = 0.5 - 0.5 * jnp.tanh(zh)
    if masked:
        r = lax.broadcasted_iota(I32, (TILE, LANES), 0)
        c = lax.broadcasted_iota(I32, (TILE, LANES), 1) + lane0
        key = ((r & (SUBLANES - 1)) * SEG) + (r >> 3)
        om = jnp.where(key < c, om, 1.0)
    run = jnp.ones((SUBLANES, LANES), F32)
    parts = [None] * SEG
    for a in reversed(range(SEG)):
        nxt = run * om[a * SUBLANES:(a + 1) * SUBLANES]
        parts[a] = run - nxt
        run = nxt
    sub = lax.broadcasted_iota(I32, (SUBLANES, LANES), 0)
    y = run
    for k in (1, 2, 4):
        y = y * jnp.where(sub + k < SUBLANES, pltpu.roll(y, SUBLANES - k, 0), 1.0)
    off = carry * jnp.where(sub + 1 < SUBLANES, pltpu.roll(y, SUBLANES - 1, 0), 1.0)
    new_carry = carry * jnp.broadcast_to(y[0:1, :], (SUBLANES, LANES))
    a_t = jnp.concatenate([p * off for p in parts], axis=0).astype(BF16)
    return a_t, new_carry


def _sb_tile(z, carry, masked):
    outs = []
    carries = []
    for lh in range(TILE // LANES):
        ls = slice(lh * LANES, (lh + 1) * LANES)
        a, c = _sb_half(z[:, ls], carry[:, ls], lh * LANES, masked)
        outs.append(a)
        carries.append(c)
    return jnp.concatenate(outs, axis=1), jnp.concatenate(carries, axis=1)


def _attn_prompt_kernel(nt_ref, k_ref, qt_ref, vt_ref, o_ref, zbuf, abuf):
    half = SB_HEAD_DIM
    row = lax.broadcasted_iota(I32, (LANES, TILE), 0)

    def k_tile(j):
        return k_ref[pl.ds(pl.multiple_of(j * TILE, TILE), TILE), :]

    def q_tile(i, _):
        qt = qt_ref[i].astype(F32)
        qh = (jnp.where(row < half, qt, 0.0).astype(BF16),
              jnp.where(row >= half, qt, 0.0).astype(BF16))
        ones = jnp.ones((SUBLANES, TILE), F32)

        kd = k_tile(i)
        carries = []
        for h in range(2):
            z = jnp.dot(kd, qh[h], preferred_element_type=F32)
            a, c = _sb_tile(z, ones, True)
            abuf[0, h] = a
            abuf[1, h] = jnp.zeros((TILE, TILE), BF16)
            carries.append(c)
        k0 = k_tile(jnp.maximum(i - 1, 0))
        k1 = k_tile(jnp.maximum(i - 2, 0))
        for h in range(2):
            zbuf[0, h] = jnp.dot(k0, qh[h], preferred_element_type=F32)
            zbuf[1, h] = jnp.dot(k1, qh[h], preferred_element_type=F32)

        def weighted_values(acc, v0, v1):
            out = []
            for h in range(2):
                rows = slice(h * half, (h + 1) * half)
                out.append(acc[h]
                           + jnp.dot(v0[rows], abuf[0, h], preferred_element_type=F32)
                           + jnp.dot(v1[rows], abuf[1, h], preferred_element_type=F32))
            return out

        def trip(st):
            t, _, c0, c1, acc0, acc1, p0, p1 = st
            j0 = i - 1 - 2 * t
            j1 = jnp.maximum(j0 - 1, 0)
            acc = weighted_values((acc0, acc1), vt_ref[p0], vt_ref[p1])
            cs = [c0, c1]
            for h in range(2):
                for s in range(2):
                    a, cs[h] = _sb_tile(zbuf[s, h], cs[h], False)
                    abuf[s, h] = a
            n0 = k_tile(jnp.maximum(j0 - 2, 0))
            n1 = k_tile(jnp.maximum(j0 - 3, 0))
            for h in range(2):
                zbuf[0, h] = jnp.dot(n0, qh[h], preferred_element_type=F32)
                zbuf[1, h] = jnp.dot(n1, qh[h], preferred_element_type=F32)
            alive = jnp.maximum(jnp.max(cs[0]), jnp.max(cs[1])) > 0.0
            return t + 1, alive, cs[0], cs[1], acc[0], acc[1], j0, j1

        zeros = jnp.zeros((half, TILE), F32)
        trips = (i + 1) // 2
        st = lax.while_loop(lambda s: (s[0] < trips) & s[1], trip,
                            (jnp.int32(0), jnp.bool_(True), carries[0], carries[1],
                             zeros, zeros, i, i))
        keep1 = (i == 0) | ((i & 1) == 0) | (st[0] < trips)
        v1 = jnp.where(keep1, vt_ref[st[7]].astype(F32), 0.0).astype(BF16)
        acc = weighted_values((st[4], st[5]), vt_ref[st[6]], v1)
        out_t = jnp.concatenate(acc, axis=0)
        o_ref[pl.ds(pl.multiple_of(i * TILE, TILE), TILE), :] = out_t.T.astype(BF16)
        return 0

    lax.fori_loop(0, nt_ref[0], q_tile, 0)


def _attn_prompt(kp, qt3, vt3, *, batch=BATCH, n_tiles=SEQ_TILES):
    rows = n_tiles * TILE
    return pl.pallas_call(
        _attn_prompt_kernel,
        grid=(batch, SB_HEADS // 2),
        in_specs=[pl.BlockSpec(memory_space=pltpu.SMEM),
                  pl.BlockSpec((rows, LANES), lambda p, g: (p, g)),
                  pl.BlockSpec((n_tiles, LANES, TILE), lambda p, g: (p, g, 0)),
                  pl.BlockSpec((n_tiles, LANES, TILE), lambda p, g: (p, g, 0))],
        out_specs=pl.BlockSpec((rows, LANES), lambda p, g: (p, g)),
        out_shape=jax.ShapeDtypeStruct((batch * rows, SB_WIDTH), BF16),
        scratch_shapes=[pltpu.VMEM((2, 2, TILE, TILE), F32),
                        pltpu.VMEM((2, 2, TILE, TILE), BF16)],
        compiler_params=_params(("arbitrary", "arbitrary")),
        name="attn_prompt",
    )(jnp.full((1,), n_tiles, I32), kp, qt3, vt3)


def _rev_excl_cumprod(om, scr):
    n = om.shape[0]
    sub = lax.broadcasted_iota(I32, (n, LANES), 0) & (SUBLANES - 1)
    scr[n:n + SUBLANES, :] = jnp.ones((SUBLANES, LANES), F32)
    y = om
    for k in (1, 2, 4):
        scr[0:n, :] = y
        y = y * jnp.where(sub + k < SUBLANES, scr[k:n + k, :], 1.0)
    scr[0:n, :] = y
    ex = jnp.where(sub + 1 < SUBLANES, scr[1:n + 1, :], 1.0)
    run = jnp.ones((SUBLANES, LANES), F32)
    out = [None] * (n // SUBLANES)
    for a in reversed(range(n // SUBLANES)):
        sl = slice(a * SUBLANES, (a + 1) * SUBLANES)
        out[a] = ex[sl] * run
        run = run * jnp.broadcast_to(y[a * SUBLANES:a * SUBLANES + 1, :], (SUBLANES, LANES))
    return jnp.concatenate(out, axis=0)


def _attn_sample_kernel(ck_ref, cv_ref, kn_ref, vn_ref, qbd_ref, o_ref, scr):
    qbd = qbd_ref[0]
    pad = jnp.zeros((LANES - DEC_SEQ, SB_WIDTH), F32)
    kn = jnp.concatenate([kn_ref[...], pad], axis=0).astype(BF16)
    vn = jnp.concatenate([vn_ref[...], pad], axis=0).astype(BF16)
    zc = jnp.dot(ck_ref[0].astype(BF16), qbd, preferred_element_type=F32)
    zn = jnp.dot(kn, qbd, preferred_element_type=F32)
    z = jnp.concatenate([zc, zn], axis=0)
    n = PAST_LEN + LANES
    r = lax.broadcasted_iota(I32, (n, LANES), 0)
    c = lax.broadcasted_iota(I32, (n, LANES), 1)
    valid = (r < PAST_LEN) | ((r - PAST_LEN) < (c & (DEC_SEQ - 1)))
    e = jnp.exp(z)
    om = 1.0 / (1.0 + e)
    beta = jnp.where(valid, 1.0 - om, 0.0)
    om = jnp.where(valid, om, 1.0)
    a_t = beta * _rev_excl_cumprod(om, scr)
    ac = a_t[0:PAST_LEN].T.astype(BF16)
    an = a_t[PAST_LEN:n].T.astype(BF16)
    p = jnp.dot(ac, cv_ref[0].astype(BF16), preferred_element_type=F32)
    p = p + jnp.dot(an, vn, preferred_element_type=F32)
    pr = lax.broadcasted_iota(I32, (LANES, SB_WIDTH), 0)
    pc = lax.broadcasted_iota(I32, (LANES, SB_WIDTH), 1)
    p = jnp.where((pr >> 4) == (pc >> 6), p, 0.0)
    out = p[0:DEC_SEQ]
    for h in range(1, SB_HEADS):
        out = out + p[h * DEC_SEQ:(h + 1) * DEC_SEQ]
    o_ref[...] = out.astype(BF16)


def _attn_sample(ck, cv, k, v, qbd, layer):
    new = lambda s: (layer, ROWS_P // DEC_SEQ + s, 0)
    per = lambda s: (s, 0, 0)
    return pl.pallas_call(
        _attn_sample_kernel,
        grid=(DEC_BATCH,),
        in_specs=[pl.BlockSpec((1, PAST_LEN, SB_WIDTH), per),
                  pl.BlockSpec((1, PAST_LEN, SB_WIDTH), per),
                  pl.BlockSpec((None, DEC_SEQ, SB_WIDTH), new),
                  pl.BlockSpec((None, DEC_SEQ, SB_WIDTH), new),
                  pl.BlockSpec((1, SB_WIDTH, LANES), per)],
        out_specs=pl.BlockSpec((DEC_SEQ, SB_WIDTH), lambda s: (s, 0)),
        out_shape=jax.ShapeDtypeStruct((ROWS_S, SB_WIDTH), BF16),
        scratch_shapes=[pltpu.VMEM((PAST_LEN + LANES + SUBLANES, LANES), F32)],
        compiler_params=_params(("arbitrary",)),
        name="attn_sample",
    )(ck, cv, k, v, qbd)


def _pool_kernel(*refs, tm, use_halo):
    if use_halo:
        pv_ref, halo_ref, w_ref, sc_ref, o_ref, s1, s2, s4, s8 = refs
    else:
        pv_ref, w_ref, sc_ref, o_ref, s1, s2, s4, s8 = refs
    i = pl.program_id(0)
    off = SUBLANES
    n = HALO + tm
    x = pv_ref[...]
    zero8 = jnp.zeros((off, POOL_WIDTH), F32)
    if use_halo:
        first = (i % (SEQ_PAD // tm)) == 0
        halo = jnp.where(first, 0.0, halo_ref[...])
    else:
        halo = jnp.zeros((HALO, POOL_WIDTH), F32)
    s1[0:off, :] = zero8
    s1[off:off + HALO, :] = halo
    s1[off + HALO:off + n, :] = x
    g = POOL_GROUP_DIM
    t2 = s1[off:off + n, :] + s1[off - 1:off - 1 + n, :]
    s2[0:off, :] = zero8[:, g:]
    s2[off:off + n, :] = t2[:, g:]
    t4 = t2[:, g:] + s2[off - 2:off - 2 + n, :]
    s4[0:off, :] = zero8[:, 2 * g:]
    s4[off:off + n, :] = t4[:, g:]
    t8 = t4[:, g:] + s4[off - 4:off - 4 + n, :]
    s8[0:off, :] = zero8[:, 3 * g:]
    s8[off:off + n, :] = t8[:, g:]
    t16 = t8[:, g:] + s8[off - 8:off - 8 + n, :]
    sums = (t2[HALO:, 0:g], t4[HALO:, 0:g], t8[HALO:, 0:g], t16[HALO:, :])
    if use_halo:
        idx = (i % (SEQ_PAD // tm)) * tm - FRONT + lax.broadcasted_iota(I32, (tm, g), 0)
    for gi, win in enumerate(POOL_WINDOWS):
        if use_halo:
            cnt = jnp.clip(idx + 1, 1, win).astype(F32)
        else:
            cnt = jnp.full((tm, g), float(win), F32)
        d = sums[gi] / cnt - x[:, gi * g:(gi + 1) * g]
        y = jnp.dot(d.astype(BF16), w_ref[gi], preferred_element_type=F32)
        o_ref[:, gi * g:(gi + 1) * g] = (y * sc_ref[:, gi * g:(gi + 1) * g]).astype(BF16)


def _pool(pv, w, scale, *, rows, tm, use_halo):
    g = POOL_GROUP_DIM
    in_specs = [pl.BlockSpec((tm, POOL_WIDTH), lambda i: (i, 0))]
    args = [pv]
    if use_halo:
        in_specs.append(pl.BlockSpec((HALO, POOL_WIDTH),
                                     lambda i: (jnp.maximum(i * (tm // HALO) - 1, 0), 0)))
        args.append(pv)
    in_specs += [pl.BlockSpec((4, g, g), lambda i: (0, 0, 0)),
                 pl.BlockSpec((1, POOL_WIDTH), lambda i: (0, 0))]
    args += [w, scale]
    n = SUBLANES + HALO + tm
    return pl.pallas_call(
        functools.partial(_pool_kernel, tm=tm, use_halo=use_halo),
        grid=(rows // tm,),
        in_specs=in_specs,
        out_specs=pl.BlockSpec((tm, POOL_WIDTH), lambda i: (i, 0)),
        out_shape=jax.ShapeDtypeStruct((rows, POOL_WIDTH), BF16),
        scratch_shapes=[pltpu.VMEM((n, 4 * g), F32), pltpu.VMEM((n, 3 * g), F32),
                        pltpu.VMEM((n, 2 * g), F32), pltpu.VMEM((n, g), F32)],
        compiler_params=_params(("arbitrary",)),
        name="pool_prompt" if use_halo else "pool_sample",
    )(*args)


def _mix_kernel(*refs, route):
    if route:
        (h_ref, pp_ref, ps_ref, sp_ref, ss_ref, w_ref, g_ref, r_ref,
         hmid_ref, hn_ref, rt_ref) = refs
    else:
        h_ref, pp_ref, ps_ref, sp_ref, ss_ref, w_ref, g_ref, hmid_ref, hn_ref = refs
    prompt = pl.program_id(0) < ROWS_P // TM
    pool = jnp.where(prompt, pp_ref[...], ps_ref[...])
    sb = jnp.where(prompt, sp_ref[...], ss_ref[...])
    hmid = (h_ref[...]
            + jnp.dot(pool, w_ref[0:POOL_WIDTH, :], preferred_element_type=F32)
            + jnp.dot(sb, w_ref[POOL_WIDTH:, :], preferred_element_type=F32))
    hmid_ref[...] = hmid
    hn = _rms(hmid, g_ref[...])
    hn_ref[...] = hn.astype(hn_ref.dtype)
    if route:
        tm = hn.shape[0]
        h_hi = hn.astype(BF16)
        h_lo = (hn - h_hi.astype(F32)).astype(BF16)
        r = r_ref[...]
        r_hi = r.astype(BF16)
        r_lo = (r - r_hi.astype(F32)).astype(BF16)
        logits = (jnp.dot(h_hi, r_hi, preferred_element_type=F32)
                  + jnp.dot(h_hi, r_lo, preferred_element_type=F32)
                  + jnp.dot(h_lo, r_hi, preferred_element_type=F32))
        lane = lax.broadcasted_iota(I32, (tm, LANES), 1)
        neg = jnp.float32(-jnp.inf)
        logits = jnp.where(lane < N_EXPERTS, logits, neg)
        m1 = jnp.max(logits, axis=-1, keepdims=True)
        i1 = jnp.min(jnp.where(logits == m1, lane, LANES), axis=-1, keepdims=True)
        rest = jnp.where(lane == i1, neg, logits)
        m2 = jnp.max(rest, axis=-1, keepdims=True)
        i2 = jnp.min(jnp.where(rest == m2, lane, LANES), axis=-1, keepdims=True)
        t = jnp.exp(m2 - m1)
        g1 = 1.0 / (1.0 + t)
        g2 = t / (1.0 + t)
        rt_ref[...] = (jnp.where(lane == 0, i1.astype(F32), 0.0)
                       + jnp.where(lane == 1, i2.astype(F32), 0.0)
                       + jnp.where(lane == 2, g1, 0.0) + jnp.where(lane == 3, g2, 0.0))


def _mix(h, pool_p, pool_s, sb_p, sb_s, w, gain, router=None):
    row = lambda i: (i, 0)
    fixed = lambda i: (0, 0)
    prow = lambda i: (jnp.minimum(i, ROWS_P // TM - 1), 0)
    route = router is not None
    in_specs = [pl.BlockSpec((TM, D_MODEL), row),
                pl.BlockSpec((TM, POOL_WIDTH), prow),
                pl.BlockSpec((TM, POOL_WIDTH), fixed),
                pl.BlockSpec((TM, SB_WIDTH), prow),
                pl.BlockSpec((TM, SB_WIDTH), fixed),
                pl.BlockSpec((D_MODEL, D_MODEL), fixed),
                pl.BlockSpec((1, D_MODEL), fixed)]
    out_specs = [pl.BlockSpec((TM, D_MODEL), row), pl.BlockSpec((TM, D_MODEL), row)]
    out_shape = [jax.ShapeDtypeStruct((ROWS, D_MODEL), F32),
                 jax.ShapeDtypeStruct((ROWS, D_MODEL), F32 if route else BF16)]
    args = [h, pool_p, pool_s, sb_p, sb_s, w, gain]
    if route:
        in_specs.append(pl.BlockSpec((D_MODEL, LANES), fixed))
        out_specs.append(pl.BlockSpec((TM, LANES), row))
        out_shape.append(jax.ShapeDtypeStruct((ROWS, LANES), F32))
        args.append(router)
    return pl.pallas_call(
        functools.partial(_mix_kernel, route=route),
        grid=(ROWS // TM,),
        in_specs=in_specs, out_specs=out_specs, out_shape=out_shape,
        compiler_params=_params(("arbitrary",)),
        name="mix_route" if route else "mix",
    )(*args)


def _silu_mul(g, u):
    return g * (1.0 / (1.0 + jnp.exp(-g))) * u


FF_CHUNK = 256


def _ffn_kernel(hmid_ref, hn_ref, wg_ref, wu_ref, wd_ref, o_ref, act):
    hn = hn_ref[...]
    for c in range(D_FF // FF_CHUNK):
        sl = slice(c * FF_CHUNK, (c + 1) * FF_CHUNK)
        g = jnp.dot(hn, wg_ref[:, sl], preferred_element_type=F32)
        u = jnp.dot(hn, wu_ref[:, sl], preferred_element_type=F32)
        act[:, sl] = _silu_mul(g, u).astype(BF16)
    o_ref[...] = hmid_ref[...] + jnp.dot(act[...], wd_ref[...], preferred_element_type=F32)


def _ffn(hmid, hn, wg, wu, wd):
    row = lambda i: (i, 0)
    fixed = lambda i: (0, 0)
    once = pl.Buffered(1)
    return pl.pallas_call(
        _ffn_kernel,
        grid=(ROWS // TM,),
        in_specs=[pl.BlockSpec((TM, D_MODEL), row),
                  pl.BlockSpec((TM, D_MODEL), row),
                  pl.BlockSpec((D_MODEL, D_FF), fixed, pipeline_mode=once),
                  pl.BlockSpec((D_MODEL, D_FF), fixed, pipeline_mode=once),
                  pl.BlockSpec((D_FF, D_MODEL), fixed, pipeline_mode=once)],
        out_specs=pl.BlockSpec((TM, D_MODEL), row),
        out_shape=jax.ShapeDtypeStruct((ROWS, D_MODEL), F32),
        scratch_shapes=[pltpu.VMEM((TM, D_FF), BF16)],
        compiler_params=_params(("arbitrary",)),
        name="ffn",
    )(hmid, hn, wg, wu, wd)


def _route_plan(route):
    e_flat = jnp.concatenate([route[:, 0], route[:, 1]]).astype(I32)
    order = jnp.argsort(e_flat).astype(I32)
    ex = jnp.arange(N_EXPERTS, dtype=I32)
    counts = jnp.sum((e_flat[:, None] == ex[None, :]).astype(I32), axis=0)
    gstart = jnp.cumsum(counts) - counts
    padded = ((counts + TM_E - 1) // TM_E) * TM_E
    pend = jnp.cumsum(padded)
    pstart = pend - padded
    q = jnp.arange(P_MAX, dtype=I32)
    e_q = jnp.sum((q[:, None] >= pend[None, :]).astype(I32), axis=1)
    e_c = jnp.minimum(e_q, N_EXPERTS - 1)
    within = q - pstart[e_c]
    valid = (e_q < N_EXPERTS) & (within < counts[e_c])
    pair = order[jnp.clip(gstart[e_c] + within, 0, PAIRS - 1)]
    src = jnp.where(valid, jnp.where(pair >= ROWS, pair - ROWS, pair), 0)
    spare = PAIRS + ((q // TM_E) % 2) * TM_E + (q % TM_E)
    dst = jnp.where(valid, pair, spare)
    tile_expert = e_c[::TM_E]
    n_used = (pend[-1] // TM_E).reshape(1)
    return (src.reshape(NT_E, 1, TM_E), dst.reshape(NT_E, 1, TM_E), tile_expert, n_used)


DMA_UNROLL = 8


def _row_dmas(per, make):
    def issue(r8, _):
        for u in range(DMA_UNROLL):
            make(r8 * DMA_UNROLL + u).start(priority=u % 2)
        return 0

    lax.fori_loop(0, per // DMA_UNROLL, issue, 0)


def _experts_kernel(te_ref, nu_ref, src0_ref, srcn_ref, dst_ref, hn_ref, wg_ref, wu_ref, wd_ref,
                    slots_ref, xbuf, ybuf, act, gsem, ssem):
    t = pl.program_id(0)
    c = pl.program_id(1)
    n_used = nu_ref[0]
    used = t < n_used
    slot = t % 2
    last = pl.num_programs(1) - 1

    def fetch(idx_ref, s):
        _row_dmas(TM_E, lambda r: pltpu.make_async_copy(
            hn_ref.at[pl.ds(idx_ref[0, 0, r], 1)], xbuf.at[s, pl.ds(r, 1)], gsem.at[s]))

    def fetch_wait(s):
        pltpu.make_async_copy(hn_ref.at[pl.ds(0, TM_E)], xbuf.at[s], gsem.at[s]).wait()

    def send(s):
        _row_dmas(TM_E, lambda r: pltpu.make_async_copy(
            ybuf.at[s, pl.ds(r, 1)], slots_ref.at[pl.ds(dst_ref[0, 0, r], 1)], ssem.at[s]))

    def send_wait(s):
        pltpu.make_async_copy(ybuf.at[s], slots_ref.at[pl.ds(0, TM_E)], ssem.at[s]).wait()

    @pl.when((t == 0) & (c == 0))
    def _():
        ybuf[0] = jnp.zeros((TM_E, D_MODEL), F32)
        spare = [pltpu.make_async_copy(ybuf.at[0], slots_ref.at[pl.ds(PAIRS + k * TM_E, TM_E)],
                                       ssem.at[0])
                 for k in range((SLOT_ROWS - PAIRS) // TM_E)]
        for cp in spare:
            cp.start()
        for cp in spare:
            cp.wait()
        fetch(src0_ref, 0)

    @pl.when(used & (c == 0))
    def _():
        fetch_wait(slot)

        @pl.when(t + 1 < n_used)
        def _():
            fetch(srcn_ref, 1 - slot)

    @pl.when(used)
    def _():
        x = xbuf[slot].astype(BF16)
        for s in range(EX_CHUNK // FF_CHUNK):
            sl = slice(s * FF_CHUNK, (s + 1) * FF_CHUNK)
            g = jnp.dot(x, wg_ref[0, :, sl], preferred_element_type=F32)
            u = jnp.dot(x, wu_ref[0, :, sl], preferred_element_type=F32)
            act[:, sl] = _silu_mul(g, u).astype(BF16)
        y = jnp.dot(act[...], wd_ref[0], preferred_element_type=F32)

        @pl.when(c == 0)
        def _():
            @pl.when(t >= 2)
            def _():
                send_wait(slot)
            ybuf[slot] = y

        @pl.when(c > 0)
        def _():
            ybuf[slot] += y

        @pl.when(c == last)
        def _():
            send(slot)

    @pl.when((t == pl.num_programs(0) - 1) & (c == last))
    def _():
        send_wait((n_used - 1) % 2)

        @pl.when(n_used >= 2)
        def _():
            send_wait(n_used % 2)


def _experts(tile_expert, n_used, src, dst, hn, wg, wu, wd):
    n_chunks = D_EXPERT // EX_CHUNK
    assert n_chunks >= 2

    def held(t, c, nu):
        live = t < nu[0]
        return jnp.where(live, t, nu[0] - 1), jnp.where(live, c, n_chunks - 1)

    def w_in_map(t, c, te, nu):
        tt, cc = held(t, c, nu)
        return te[tt], 0, cc

    def w_out_map(t, c, te, nu):
        tt, cc = held(t, c, nu)
        return te[tt], cc, 0

    idx_block = lambda fn: pl.BlockSpec((1, 1, TM_E), fn, memory_space=pltpu.SMEM)
    grid_spec = pltpu.PrefetchScalarGridSpec(
        num_scalar_prefetch=2,
        grid=(NT_E, n_chunks),
        in_specs=[idx_block(lambda t, c, te, nu: (0, 0, 0)),
                  idx_block(lambda t, c, te, nu: (jnp.minimum(t + 1, NT_E - 1), 0, 0)),
                  idx_block(lambda t, c, te, nu: (t, 0, 0)),
                  pl.BlockSpec(memory_space=pl.ANY),
                  pl.BlockSpec((1, D_MODEL, EX_CHUNK), w_in_map),
                  pl.BlockSpec((1, D_MODEL, EX_CHUNK), w_in_map),
                  pl.BlockSpec((1, EX_CHUNK, D_MODEL), w_out_map)],
        out_specs=pl.BlockSpec(memory_space=pl.ANY),
        scratch_shapes=[pltpu.VMEM((2, TM_E, D_MODEL), F32), pltpu.VMEM((2, TM_E, D_MODEL), F32),
                        pltpu.VMEM((TM_E, EX_CHUNK), BF16),
                        pltpu.SemaphoreType.DMA((2,)), pltpu.SemaphoreType.DMA((2,))])
    return pl.pallas_call(
        _experts_kernel,
        grid_spec=grid_spec,
        out_shape=jax.ShapeDtypeStruct((SLOT_ROWS, D_MODEL), F32),
        compiler_params=_params(("arbitrary", "arbitrary")),
        name="experts",
    )(tile_expert, n_used, src, src, dst, hn, wg, wu, wd)


def _combine_kernel(hmid_ref, a_ref, b_ref, rt_ref, fg_ref, yp_ref, ys_ref):
    i = pl.program_id(0)
    lane = lax.broadcasted_iota(I32, rt_ref.shape, 1)
    rt = rt_ref[...]
    g1 = jnp.sum(jnp.where(lane == 2, rt, 0.0), axis=-1, keepdims=True)
    g2 = jnp.sum(jnp.where(lane == 3, rt, 0.0), axis=-1, keepdims=True)
    y = _rms(hmid_ref[...] + g1 * a_ref[...] + g2 * b_ref[...], fg_ref[...])
    prompt = i < ROWS_P // TILE

    @pl.when(prompt & (i % SEQ_TILES != 0))
    def _():
        yp_ref[0] = y

    @pl.when(jnp.logical_not(prompt))
    def _():
        ys_ref[...] = y


def _combine(hmid, slots, route, fgain):
    n_p = ROWS_P // TILE

    def yp_map(i):
        ip = jnp.minimum(i, n_p - 1)
        return (ip // SEQ_TILES, jnp.maximum(ip % SEQ_TILES - 1, 0), 0)

    return pl.pallas_call(
        _combine_kernel,
        grid=(ROWS // TILE,),
        in_specs=[pl.BlockSpec((TILE, D_MODEL), lambda i: (i, 0)),
                  pl.BlockSpec((TILE, D_MODEL), lambda i: (i, 0)),
                  pl.BlockSpec((TILE, D_MODEL), lambda i: (i + ROWS // TILE, 0)),
                  pl.BlockSpec((TILE, LANES), lambda i: (i, 0)),
                  pl.BlockSpec((1, D_MODEL), lambda i: (0, 0))],
        out_specs=[pl.BlockSpec((1, TILE, D_MODEL), yp_map),
                   pl.BlockSpec((TILE, D_MODEL), lambda i: (jnp.maximum(i - n_p, 0), 0))],
        out_shape=[jax.ShapeDtypeStruct((BATCH, SEQ, D_MODEL), F32),
                   jax.ShapeDtypeStruct((ROWS_S, D_MODEL), F32)],
        compiler_params=_params(("arbitrary",)),
        name="combine",
    )(hmid, slots, slots, route, fgain)


def _moe(hmid, hn, route, wg, wu, wd, fgain):
    src, dst, tile_expert, n_used = _route_plan(route)
    slots = _experts(tile_expert, n_used, src, dst, hn, wg, wu, wd)
    return _combine(hmid, slots, route, fgain)


def _sample_qbd(q):
    qs = q[ROWS_P:].reshape(DEC_BATCH, DEC_SEQ, SB_HEADS, SB_HEAD_DIM)
    eye = jnp.eye(SB_HEADS, dtype=q.dtype)
    qbd = jnp.einsum('sthd,hg->shdgt', qs, eye)
    return qbd.reshape(DEC_BATCH, SB_WIDTH, SB_HEADS * DEC_SEQ)


def kernel(x_prompt, x_sample, cache_k, cache_v, state_pool, meta_tokens, norm_mix, w_in, pool_w, pool_scale, w_out, norm_ffn, ffn_w_gate, ffn_w_up, ffn_w_down, moe_router, moe_w_gate, moe_w_up, moe_w_down, final_norm):
    h = _token_rows(x_prompt, meta_tokens, x_sample.reshape(ROWS_S, D_MODEL))
    perm = _key_permutation()

    outs = {name: [] for name in ("pp", "sp")}
    y_prompt = y_sample = None
    kv_bufs = (jnp.zeros((DEPTH, ROWS, 512), F32), jnp.zeros((DEPTH, ROWS, 512), F32))
    for l in range(DEPTH):
        pv, q, k, v, kp, qt3, vt3 = _inproj(h, norm_mix[l][None], w_in[l].astype(BF16), perm,
                                            l, kv_bufs)
        kv_bufs = (k, v)

        sb_p = _attn_prompt(kp, qt3, vt3)
        sb_s = _attn_sample(cache_k[l].reshape(DEC_BATCH, PAST_LEN, SB_WIDTH),
                            cache_v[l].reshape(DEC_BATCH, PAST_LEN, SB_WIDTH),
                            k, v, _sample_qbd(q), l)

        pw = pool_w[l].astype(BF16)
        ps = pool_scale[l][None]
        pool_p = _pool(pv, pw, ps, rows=ROWS_P, tm=TILE, use_halo=True)
        pv_s = pv[ROWS_P:].reshape(DEC_BATCH, DEC_SEQ, POOL_WIDTH)
        pbuf = jnp.concatenate([jnp.zeros((DEC_BATCH, 1, POOL_WIDTH), F32), state_pool[l], pv_s],
                               axis=1).reshape(DEC_BATCH * 2 * DEC_SEQ, POOL_WIDTH)
        pool_s = _pool(pbuf, pw, ps, rows=DEC_BATCH * 2 * DEC_SEQ, tm=TM, use_halo=False)
        pool_s = pool_s.reshape(DEC_BATCH, 2 * DEC_SEQ, POOL_WIDTH)[:, DEC_SEQ:].reshape(ROWS_S, POOL_WIDTH)

        wo = w_out[l].astype(BF16)
        j = l // 2
        if l % 2 == 0:
            hmid, hn = _mix(h, pool_p, pool_s, sb_p, sb_s, wo, norm_ffn[l][None])
            h = _ffn(hmid, hn, ffn_w_gate[j].astype(BF16), ffn_w_up[j].astype(BF16),
                     ffn_w_down[j].astype(BF16))
        else:
            router = jnp.pad(moe_router[j], ((0, 0), (0, LANES - N_EXPERTS)))
            hmid, hn, route = _mix(h, pool_p, pool_s, sb_p, sb_s, wo, norm_ffn[l][None], router)
            y_prompt, y_sample = _moe(hmid, hn, route, moe_w_gate[j].astype(BF16),
                                      moe_w_up[j].astype(BF16), moe_w_down[j].astype(BF16),
                                      final_norm[None])

        outs["pp"].append(pv[:ROWS_P].reshape(BATCH, SEQ_PAD, POOL_WIDTH)[:, -POOL_STATE:])
        outs["sp"].append(pv_s[:, -POOL_STATE:])

    def prompt_rows(buf):
        rows = buf[:, :ROWS_P].reshape(DEPTH, BATCH, SEQ_PAD, SB_HEADS, SB_HEAD_DIM)
        return rows[:, :, FRONT:]

    def sample_rows(buf):
        return buf[:, ROWS_P:].reshape(DEPTH, DEC_BATCH, DEC_SEQ, SB_HEADS, SB_HEAD_DIM)

    k, v = kv_bufs
    y_sample = y_sample.reshape(DEC_BATCH, DEC_SEQ, D_MODEL)
    return (y_prompt, y_sample, prompt_rows(k), prompt_rows(v), jnp.stack(outs["pp"]),
            sample_rows(k), sample_rows(v), jnp.stack(outs["sp"]))
```

```python
import functools

import jax
import jax.numpy as jnp
from jax import lax
from jax.experimental import pallas as pl
from jax.experimental.pallas import tpu as pltpu

F32 = jnp.float32
BF16 = jnp.bfloat16
I32 = jnp.int32

D_MODEL = 1024
BATCH = 2
SEQ = 8192
DEPTH = 2
DEC_BATCH = 32
DEC_SEQ = 16
PAST_LEN = 1024
N_META = 16
POOL_WIDTH = 512
POOL_WINDOWS = (2, 4, 8, 16)
POOL_GROUP_DIM = 128
POOL_STATE = 15
SB_HEADS = 8
SB_HEAD_DIM = 64
SB_WIDTH = 512
IN_WIDTH = 2048
D_FF = 2816
N_EXPERTS = 8
D_EXPERT = 3584
EPS = 1e-6

LANES = 128
SUBLANES = 8
TILE = 256
SEG = TILE // SUBLANES
FRONT = TILE - N_META
SEQ_PAD = FRONT + N_META + SEQ
SEQ_TILES = SEQ_PAD // TILE
ROWS_P = BATCH * SEQ_PAD
ROWS_S = DEC_BATCH * DEC_SEQ
ROWS = ROWS_P + ROWS_S
TM = 512
HALO = 16
VMEM_LIMIT = 56 * 1024 * 1024

TM_E = 512
PAIRS = 2 * ROWS
P_MAX = PAIRS + N_EXPERTS * TM_E
NT_E = P_MAX // TM_E
SLOT_ROWS = PAIRS + 2 * TM_E
EX_CHUNK = 1792


def _params(sem, vmem=VMEM_LIMIT):
    return pltpu.CompilerParams(dimension_semantics=sem, vmem_limit_bytes=vmem)


def _rms(x, gain):
    ms = jnp.mean(x * x, axis=-1, keepdims=True)
    return x * lax.rsqrt(ms + EPS) * gain


def _token_rows_kernel(xp_ref, meta_ref, xs_ref, o_ref):
    i = pl.program_id(0)
    t = i % SEQ_TILES
    prompt = i < ROWS_P // TILE

    @pl.when(prompt & (t == 0))
    def _():
        o_ref[...] = jnp.concatenate([jnp.zeros((FRONT, D_MODEL), F32), meta_ref[...]], axis=0)

    @pl.when(prompt & (t > 0))
    def _():
        o_ref[...] = xp_ref[0]

    @pl.when(jnp.logical_not(prompt))
    def _():
        o_ref[...] = xs_ref[...]


def _token_rows(x_prompt, meta, x_sample):
    n_p = ROWS_P // TILE

    def xp_map(i):
        ip = jnp.minimum(i, n_p - 1)
        return ip // SEQ_TILES, jnp.maximum(ip % SEQ_TILES - 1, 0), 0

    return pl.pallas_call(
        _token_rows_kernel,
        grid=(ROWS // TILE,),
        in_specs=[pl.BlockSpec((1, TILE, D_MODEL), xp_map),
                  pl.BlockSpec((N_META, D_MODEL), lambda i: (0, 0)),
                  pl.BlockSpec((TILE, D_MODEL), lambda i: (jnp.maximum(i - n_p, 0), 0))],
        out_specs=pl.BlockSpec((TILE, D_MODEL), lambda i: (i, 0)),
        out_shape=jax.ShapeDtypeStruct((ROWS, D_MODEL), F32),
        compiler_params=_params(("arbitrary",)),
        name="token_rows",
    )(x_prompt, meta, x_sample)


def _inproj_kernel(*refs):
    x_ref, g_ref, w_ref, perm_ref = refs[:4]
    pv_ref, q_ref, k_ref, v_ref, kp_ref, qt_ref, vt_ref = refs[-7:]
    hn = _rms(x_ref[...], g_ref[...]).astype(BF16)
    pv_ref[...] = jnp.dot(hn, w_ref[:, 0:512], preferred_element_type=F32)
    q = jnp.dot(hn, w_ref[:, 512:1024], preferred_element_type=F32) * (SB_HEAD_DIM ** -0.5)
    k = jnp.dot(hn, w_ref[:, 1024:1536], preferred_element_type=F32)
    v = jnp.dot(hn, w_ref[:, 1536:2048], preferred_element_type=F32)
    q_ref[...] = q.astype(BF16)
    k_ref[...] = k
    v_ref[...] = v
    perm = perm_ref[...]
    for half in range(TM // TILE):
        sl = slice(half * TILE, (half + 1) * TILE)
        kp_ref[sl, :] = jnp.dot(perm, k[sl].astype(BF16), preferred_element_type=F32).astype(BF16)
        vp = jnp.dot(perm, v[sl].astype(BF16), preferred_element_type=F32)
        vt_ref[half] = vp.T.astype(BF16)
        qt_ref[half] = (q[sl] * 0.5).T.astype(BF16)


def _inproj(h, gain, w, perm, layer, kv_bufs):
    row = lambda i: (i, 0)
    fixed = lambda i: (0, 0)
    tiles = lambda i: (i, 0, 0)
    lrow = lambda i: (layer, i, 0)
    per = TM // TILE
    kv_block = pl.BlockSpec((None, TM, 512), lrow)
    kv_shape = jax.ShapeDtypeStruct((DEPTH, ROWS, 512), F32)
    in_specs = [pl.BlockSpec((TM, D_MODEL), row),
                pl.BlockSpec((1, D_MODEL), fixed),
                pl.BlockSpec((D_MODEL, IN_WIDTH), fixed),
                pl.BlockSpec((TILE, TILE), fixed),
                pl.BlockSpec(memory_space=pl.ANY), pl.BlockSpec(memory_space=pl.ANY)]
    args = [h, gain, w, perm, *kv_bufs]
    aliases = {4: 2, 5: 3}
    return pl.pallas_call(
        _inproj_kernel,
        grid=(ROWS // TM,),
        in_specs=in_specs,
        out_specs=[pl.BlockSpec((TM, 512), row), pl.BlockSpec((TM, 512), row),
                   kv_block, kv_block,
                   pl.BlockSpec((TM, 512), row)]
        + [pl.BlockSpec((per, SB_WIDTH, TILE), tiles)] * 2,
        out_shape=[jax.ShapeDtypeStruct((ROWS, 512), F32),
                   jax.ShapeDtypeStruct((ROWS, 512), BF16),
                   kv_shape, kv_shape,
                   jax.ShapeDtypeStruct((ROWS, 512), BF16),
                   jax.ShapeDtypeStruct((ROWS // TILE, SB_WIDTH, TILE), BF16),
                   jax.ShapeDtypeStruct((ROWS // TILE, SB_WIDTH, TILE), BF16)],
        input_output_aliases=aliases,
        compiler_params=_params(("arbitrary",)),
        name="inproj",
    )(*args)


def _key_permutation():
    r = jnp.arange(TILE)
    src = (r % SUBLANES) * SEG + r // SUBLANES
    return (src[:, None] == jnp.arange(TILE)[None, :]).astype(BF16)


def _sb_half(zh, carry, lane0, masked):
    om = 0.5 - 0.5 * jnp.tanh(zh)
    if masked:
        r = lax.broadcasted_iota(I32, (TILE, LANES), 0)
        c = lax.broadcasted_iota(I32, (TILE, LANES), 1) + lane0
        key = ((r & (SUBLANES - 1)) * SEG) + (r >> 3)
        om = jnp.where(key < c, om, 1.0)
    run = jnp.ones((SUBLANES, LANES), F32)
    parts = [None] * SEG
    for a in reversed(range(SEG)):
        nxt = run * om[a * SUBLANES:(a + 1) * SUBLANES]
        parts[a] = run - nxt
        run = nxt
    sub = lax.broadcasted_iota(I32, (SUBLANES, LANES), 0)
    y = run
    for k in (1, 2, 4):
        y = y * jnp.where(sub + k < SUBLANES, pltpu.roll(y, SUBLANES - k, 0), 1.0)
    off = carry * jnp.where(sub + 1 < SUBLANES, pltpu.roll(y, SUBLANES - 1, 0), 1.0)
    new_carry = carry * jnp.broadcast_to(y[0:1, :], (SUBLANES, LANES))
    a_t = jnp.concatenate([p * off for p in parts], axis=0).astype(BF16)
    return a_t, new_carry


def _sb_tile(z, carry, masked):
    outs = []
    carries = []
    for lh in range(TILE // LANES):
        ls = slice(lh * LANES, (lh + 1) * LANES)
        a, c = _sb_half(z[:, ls], carry[:, ls], lh * LANES, masked)
        outs.append(a)
        carries.append(c)
    return jnp.concatenate(outs, axis=1), jnp.concatenate(carries, axis=1)


def _attn_prompt_kernel(nt_ref, k_ref, qt_ref, vt_ref, o_ref, zbuf, abuf):
    half = SB_HEAD_DIM
    row = lax.broadcasted_iota(I32, (LANES, TILE), 0)

    def k_tile(j):
        return k_ref[pl.ds(pl.multiple_of(j * TILE, TILE), TILE), :]

    def q_tile(i, _):
        qt = qt_ref[i].astype(F32)
        qh = (jnp.where(row < half, qt, 0.0).astype(BF16),
              jnp.where(row >= half, qt, 0.0).astype(BF16))
        ones = jnp.ones((SUBLANES, TILE), F32)

        kd = k_tile(i)
        carries = []
        for h in range(2):
            z = jnp.dot(kd, qh[h], preferred_element_type=F32)
            a, c = _sb_tile(z, ones, True)
            abuf[0, h] = a
            abuf[1, h] = jnp.zeros((TILE, TILE), BF16)
            carries.append(c)
        k0 = k_tile(jnp.maximum(i - 1, 0))
        k1 = k_tile(jnp.maximum(i - 2, 0))
        for h in range(2):
            zbuf[0, h] = jnp.dot(k0, qh[h], preferred_element_type=F32)
            zbuf[1, h] = jnp.dot(k1, qh[h], preferred_element_type=F32)

        def weighted_values(acc, v0, v1):
            out = []
            for h in range(2):
                rows = slice(h * half, (h + 1) * half)
                out.append(acc[h]
                           + jnp.dot(v0[rows], abuf[0, h], preferred_element_type=F32)
                           + jnp.dot(v1[rows], abuf[1, h], preferred_element_type=F32))
            return out

        def trip(st):
            t, _, c0, c1, acc0, acc1, p0, p1 = st
            j0 = i - 1 - 2 * t
            j1 = jnp.maximum(j0 - 1, 0)
            acc = weighted_values((acc0, acc1), vt_ref[p0], vt_ref[p1])
            cs = [c0, c1]
            for h in range(2):
                for s in range(2):
                    a, cs[h] = _sb_tile(zbuf[s, h], cs[h], False)
                    abuf[s, h] = a
            n0 = k_tile(jnp.maximum(j0 - 2, 0))
            n1 = k_tile(jnp.maximum(j0 - 3, 0))
            for h in range(2):
                zbuf[0, h] = jnp.dot(n0, qh[h], preferred_element_type=F32)
                zbuf[1, h] = jnp.dot(n1, qh[h], preferred_element_type=F32)
            alive = jnp.maximum(jnp.max(cs[0]), jnp.max(cs[1])) > 0.0
            return t + 1, alive, cs[0], cs[1], acc[0], acc[1], j0, j1

        zeros = jnp.zeros((half, TILE), F32)
        trips = (i + 1) // 2
        st = lax.while_loop(lambda s: (s[0] < trips) & s[1], trip,
                            (jnp.int32(0), jnp.bool_(True), carries[0], carries[1],
                             zeros, zeros, i, i))
        keep1 = (i == 0) | ((i & 1) == 0) | (st[0] < trips)
        v1 = jnp.where(keep1, vt_ref[st[7]].astype(F32), 0.0).astype(BF16)
        acc = weighted_values((st[4], st[5]), vt_ref[st[6]], v1)
        out_t = jnp.concatenate(acc, axis=0)
        o_ref[pl.ds(pl.multiple_of(i * TILE, TILE), TILE), :] = out_t.T.astype(BF16)
        return 0

    lax.fori_loop(0, nt_ref[0], q_tile, 0)


def _attn_prompt(kp, qt3, vt3, *, batch=BATCH, n_tiles=SEQ_TILES):
    rows = n_tiles * TILE
    return pl.pallas_call(
        _attn_prompt_kernel,
        grid=(batch, SB_HEADS // 2),
        in_specs=[pl.BlockSpec(memory_space=pltpu.SMEM),
                  pl.BlockSpec((rows, LANES), lambda p, g: (p, g)),
                  pl.BlockSpec((n_tiles, LANES, TILE), lambda p, g: (p, g, 0)),
                  pl.BlockSpec((n_tiles, LANES, TILE), lambda p, g: (p, g, 0))],
        out_specs=pl.BlockSpec((rows, LANES), lambda p, g: (p, g)),
        out_shape=jax.ShapeDtypeStruct((batch * rows, SB_WIDTH), BF16),
        scratch_shapes=[pltpu.VMEM((2, 2, TILE, TILE), F32),
                        pltpu.VMEM((2, 2, TILE, TILE), BF16)],
        compiler_params=_params(("arbitrary", "arbitrary")),
        name="attn_prompt",
    )(jnp.full((1,), n_tiles, I32), kp, qt3, vt3)


def _rev_excl_cumprod(om, scr):
    n = om.shape[0]
    sub = lax.broadcasted_iota(I32, (n, LANES), 0) & (SUBLANES - 1)
    scr[n:n + SUBLANES, :] = jnp.ones((SUBLANES, LANES), F32)
    y = om
    for k in (1, 2, 4):
        scr[0:n, :] = y
        y = y * jnp.where(sub + k < SUBLANES, scr[k:n + k, :], 1.0)
    scr[0:n, :] = y
    ex = jnp.where(sub + 1 < SUBLANES, scr[1:n + 1, :], 1.0)
    run = jnp.ones((SUBLANES, LANES), F32)
    out = [None] * (n // SUBLANES)
    for a in reversed(range(n // SUBLANES)):
        sl = slice(a * SUBLANES, (a + 1) * SUBLANES)
        out[a] = ex[sl] * run
        run = run * jnp.broadcast_to(y[a * SUBLANES:a * SUBLANES + 1, :], (SUBLANES, LANES))
    return jnp.concatenate(out, axis=0)


def _attn_sample_kernel(ck_ref, cv_ref, kn_ref, vn_ref, qbd_ref, o_ref, scr):
    qbd = qbd_ref[0]
    pad = jnp.zeros((LANES - DEC_SEQ, SB_WIDTH), F32)
    kn = jnp.concatenate([kn_ref[...], pad], axis=0).astype(BF16)
    vn = jnp.concatenate([vn_ref[...], pad], axis=0).astype(BF16)
    zc = jnp.dot(ck_ref[0].astype(BF16), qbd, preferred_element_type=F32)
    zn = jnp.dot(kn, qbd, preferred_element_type=F32)
    z = jnp.concatenate([zc, zn], axis=0)
    n = PAST_LEN + LANES
    r = lax.broadcasted_iota(I32, (n, LANES), 0)
    c = lax.broadcasted_iota(I32, (n, LANES), 1)
    valid = (r < PAST_LEN) | ((r - PAST_LEN) < (c & (DEC_SEQ - 1)))
    e = jnp.exp(z)
    om = 1.0 / (1.0 + e)
    beta = jnp.where(valid, 1.0 - om, 0.0)
    om = jnp.where(valid, om, 1.0)
    a_t = beta * _rev_excl_cumprod(om, scr)
    ac = a_t[0:PAST_LEN].T.astype(BF16)
    an = a_t[PAST_LEN:n].T.astype(BF16)
    p = jnp.dot(ac, cv_ref[0].astype(BF16), preferred_element_type=F32)
    p = p + jnp.dot(an, vn, preferred_element_type=F32)
    pr = lax.broadcasted_iota(I32, (LANES, SB_WIDTH), 0)
    pc = lax.broadcasted_iota(I32, (LANES, SB_WIDTH), 1)
    p = jnp.where((pr >> 4) == (pc >> 6), p, 0.0)
    out = p[0:DEC_SEQ]
    for h in range(1, SB_HEADS):
        out = out + p[h * DEC_SEQ:(h + 1) * DEC_SEQ]
    o_ref[...] = out.astype(BF16)


def _attn_sample(ck, cv, k, v, qbd, layer):
    new = lambda s: (layer, ROWS_P // DEC_SEQ + s, 0)
    per = lambda s: (s, 0, 0)
    return pl.pallas_call(
        _attn_sample_kernel,
        grid=(DEC_BATCH,),
        in_specs=[pl.BlockSpec((1, PAST_LEN, SB_WIDTH), per),
                  pl.BlockSpec((1, PAST_LEN, SB_WIDTH), per),
                  pl.BlockSpec((None, DEC_SEQ, SB_WIDTH), new),
                  pl.BlockSpec((None, DEC_SEQ, SB_WIDTH), new),
                  pl.BlockSpec((1, SB_WIDTH, LANES), per)],
        out_specs=pl.BlockSpec((DEC_SEQ, SB_WIDTH), lambda s: (s, 0)),
        out_shape=jax.ShapeDtypeStruct((ROWS_S, SB_WIDTH), BF16),
        scratch_shapes=[pltpu.VMEM((PAST_LEN + LANES + SUBLANES, LANES), F32)],
        compiler_params=_params(("arbitrary",)),
        name="attn_sample",
    )(ck, cv, k, v, qbd)


def _pool_kernel(*refs, tm, use_halo):
    if use_halo:
        pv_ref, halo_ref, w_ref, sc_ref, o_ref, s1, s2, s4, s8 = refs
    else:
        pv_ref, w_ref, sc_ref, o_ref, s1, s2, s4, s8 = refs
    i = pl.program_id(0)
    off = SUBLANES
    n = HALO + tm
    x = pv_ref[...]
    zero8 = jnp.zeros((off, POOL_WIDTH), F32)
    if use_halo:
        first = (i % (SEQ_PAD // tm)) == 0
        halo = jnp.where(first, 0.0, halo_ref[...])
    else:
        halo = jnp.zeros((HALO, POOL_WIDTH), F32)
    s1[0:off, :] = zero8
    s1[off:off + HALO, :] = halo
    s1[off + HALO:off + n, :] = x
    g = POOL_GROUP_DIM
    t2 = s1[off:off + n, :] + s1[off - 1:off - 1 + n, :]
    s2[0:off, :] = zero8[:, g:]
    s2[off:off + n, :] = t2[:, g:]
    t4 = t2[:, g:] + s2[off - 2:off - 2 + n, :]
    s4[0:off, :] = zero8[:, 2 * g:]
    s4[off:off + n, :] = t4[:, g:]
    t8 = t4[:, g:] + s4[off - 4:off - 4 + n, :]
    s8[0:off, :] = zero8[:, 3 * g:]
    s8[off:off + n, :] = t8[:, g:]
    t16 = t8[:, g:] + s8[off - 8:off - 8 + n, :]
    sums = (t2[HALO:, 0:g], t4[HALO:, 0:g], t8[HALO:, 0:g], t16[HALO:, :])
    if use_halo:
        idx = (i % (SEQ_PAD // tm)) * tm - FRONT + lax.broadcasted_iota(I32, (tm, g), 0)
    for gi, win in enumerate(POOL_WINDOWS):
        if use_halo:
            cnt = jnp.clip(idx + 1, 1, win).astype(F32)
        else:
            cnt = jnp.full((tm, g), float(win), F32)
        d = sums[gi] / cnt - x[:, gi * g:(gi + 1) * g]
        y = jnp.dot(d.astype(BF16), w_ref[gi], preferred_element_type=F32)
        o_ref[:, gi * g:(gi + 1) * g] = (y * sc_ref[:, gi * g:(gi + 1) * g]).astype(BF16)


def _pool(pv, w, scale, *, rows, tm, use_halo):
    g = POOL_GROUP_DIM
    in_specs = [pl.BlockSpec((tm, POOL_WIDTH), lambda i: (i, 0))]
    args = [pv]
    if use_halo:
        in_specs.append(pl.BlockSpec((HALO, POOL_WIDTH),
                                     lambda i: (jnp.maximum(i * (tm // HALO) - 1, 0), 0)))
        args.append(pv)
    in_specs += [pl.BlockSpec((4, g, g), lambda i: (0, 0, 0)),
                 pl.BlockSpec((1, POOL_WIDTH), lambda i: (0, 0))]
    args += [w, scale]
    n = SUBLANES + HALO + tm
    return pl.pallas_call(
        functools.partial(_pool_kernel, tm=tm, use_halo=use_halo),
        grid=(rows // tm,),
        in_specs=in_specs,
        out_specs=pl.BlockSpec((tm, POOL_WIDTH), lambda i: (i, 0)),
        out_shape=jax.ShapeDtypeStruct((rows, POOL_WIDTH), BF16),
        scratch_shapes=[pltpu.VMEM((n, 4 * g), F32), pltpu.VMEM((n, 3 * g), F32),
                        pltpu.VMEM((n, 2 * g), F32), pltpu.VMEM((n, g), F32)],
        compiler_params=_params(("arbitrary",)),
        name="pool_prompt" if use_halo else "pool_sample",
    )(*args)


def _mix_kernel(*refs, route):
    if route:
        (h_ref, pp_ref, ps_ref, sp_ref, ss_ref, w_ref, g_ref, r_ref,
         hmid_ref, hn_ref, rt_ref) = refs
    else:
        h_ref, pp_ref, ps_ref, sp_ref, ss_ref, w_ref, g_ref, hmid_ref, hn_ref = refs
    prompt = pl.program_id(0) < ROWS_P // TM
    pool = jnp.where(prompt, pp_ref[...], ps_ref[...])
    sb = jnp.where(prompt, sp_ref[...], ss_ref[...])
    hmid = (h_ref[...]
            + jnp.dot(pool, w_ref[0:POOL_WIDTH, :], preferred_element_type=F32)
            + jnp.dot(sb, w_ref[POOL_WIDTH:, :], preferred_element_type=F32))
    hmid_ref[...] = hmid
    hn = _rms(hmid, g_ref[...])
    hn_ref[...] = hn.astype(hn_ref.dtype)
    if route:
        tm = hn.shape[0]
        h_hi = hn.astype(BF16)
        h_lo = (hn - h_hi.astype(F32)).astype(BF16)
        r = r_ref[...]
        r_hi = r.astype(BF16)
        r_lo = (r - r_hi.astype(F32)).astype(BF16)
        logits = (jnp.dot(h_hi, r_hi, preferred_element_type=F32)
                  + jnp.dot(h_hi, r_lo, preferred_element_type=F32)
                  + jnp.dot(h_lo, r_hi, preferred_element_type=F32))
        lane = lax.broadcasted_iota(I32, (tm, LANES), 1)
        neg = jnp.float32(-jnp.inf)
        logits = jnp.where(lane < N_EXPERTS, logits, neg)
        m1 = jnp.max(logits, axis=-1, keepdims=True)
        i1 = jnp.min(jnp.where(logits == m1, lane, LANES), axis=-1, keepdims=True)
        rest = jnp.where(lane == i1, neg, logits)
        m2 = jnp.max(rest, axis=-1, keepdims=True)
        i2 = jnp.min(jnp.where(rest == m2, lane, LANES), axis=-1, keepdims=True)
        t = jnp.exp(m2 - m1)
        g1 = 1.0 / (1.0 + t)
        g2 = t / (1.0 + t)
        rt_ref[...] = (jnp.where(lane == 0, i1.astype(F32), 0.0)
                       + jnp.where(lane == 1, i2.astype(F32), 0.0)
                       + jnp.where(lane == 2, g1, 0.0) + jnp.where(lane == 3, g2, 0.0))


def _mix(h, pool_p, pool_s, sb_p, sb_s, w, gain, router=None):
    row = lambda i: (i, 0)
    fixed = lambda i: (0, 0)
    prow = lambda i: (jnp.minimum(i, ROWS_P // TM - 1), 0)
    route = router is not None
    in_specs = [pl.BlockSpec((TM, D_MODEL), row),
                pl.BlockSpec((TM, POOL_WIDTH), prow),
                pl.BlockSpec((TM, POOL_WIDTH), fixed),
                pl.BlockSpec((TM, SB_WIDTH), prow),
                pl.BlockSpec((TM, SB_WIDTH), fixed),
                pl.BlockSpec((D_MODEL, D_MODEL), fixed),
                pl.BlockSpec((1, D_MODEL), fixed)]
    out_specs = [pl.BlockSpec((TM, D_MODEL), row), pl.BlockSpec((TM, D_MODEL), row)]
    out_shape = [jax.ShapeDtypeStruct((ROWS, D_MODEL), F32),
                 jax.ShapeDtypeStruct((ROWS, D_MODEL), F32 if route else BF16)]
    args = [h, pool_p, pool_s, sb_p, sb_s, w, gain]
    if route:
        in_specs.append(pl.BlockSpec((D_MODEL, LANES), fixed))
        out_specs.append(pl.BlockSpec((TM, LANES), row))
        out_shape.append(jax.ShapeDtypeStruct((ROWS, LANES), F32))
        args.append(router)
    return pl.pallas_call(
        functools.partial(_mix_kernel, route=route),
        grid=(ROWS // TM,),
        in_specs=in_specs, out_specs=out_specs, out_shape=out_shape,
        compiler_params=_params(("arbitrary",)),
        name="mix_route" if route else "mix",
    )(*args)


def _silu_mul(g, u):
    return g * (1.0 / (1.0 + jnp.exp(-g))) * u


FF_CHUNK = 256


def _ffn_kernel(hmid_ref, hn_ref, wg_ref, wu_ref, wd_ref, o_ref, act):
    hn = hn_ref[...]
    for c in range(D_FF // FF_CHUNK):
        sl = slice(c * FF_CHUNK, (c + 1) * FF_CHUNK)
        g = jnp.dot(hn, wg_ref[:, sl], preferred_element_type=F32)
        u = jnp.dot(hn, wu_ref[:, sl], preferred_element_type=F32)
        act[:, sl] = _silu_mul(g, u).astype(BF16)
    o_ref[...] = hmid_ref[...] + jnp.dot(act[...], wd_ref[...], preferred_element_type=F32)


def _ffn(hmid, hn, wg, wu, wd):
    row = lambda i: (i, 0)
    fixed = lambda i: (0, 0)
    once = pl.Buffered(1)
    return pl.pallas_call(
        _ffn_kernel,
        grid=(ROWS // TM,),
        in_specs=[pl.BlockSpec((TM, D_MODEL), row),
                  pl.BlockSpec((TM, D_MODEL), row),
                  pl.BlockSpec((D_MODEL, D_FF), fixed, pipeline_mode=once),
                  pl.BlockSpec((D_MODEL, D_FF), fixed, pipeline_mode=once),
                  pl.BlockSpec((D_FF, D_MODEL), fixed, pipeline_mode=once)],
        out_specs=pl.BlockSpec((TM, D_MODEL), row),
        out_shape=jax.ShapeDtypeStruct((ROWS, D_MODEL), F32),
        scratch_shapes=[pltpu.VMEM((TM, D_FF), BF16)],
        compiler_params=_params(("arbitrary",)),
        name="ffn",
    )(hmid, hn, wg, wu, wd)


def _route_plan(route):
    e_flat = jnp.concatenate([route[:, 0], route[:, 1]]).astype(I32)
    order = jnp.argsort(e_flat).astype(I32)
    ex = jnp.arange(N_EXPERTS, dtype=I32)
    counts = jnp.sum((e_flat[:, None] == ex[None, :]).astype(I32), axis=0)
    gstart = jnp.cumsum(counts) - counts
    padded = ((counts + TM_E - 1) // TM_E) * TM_E
    pend = jnp.cumsum(padded)
    pstart = pend - padded
    q = jnp.arange(P_MAX, dtype=I32)
    e_q = jnp.sum((q[:, None] >= pend[None, :]).astype(I32), axis=1)
    e_c = jnp.minimum(e_q, N_EXPERTS - 1)
    within = q - pstart[e_c]
    valid = (e_q < N_EXPERTS) & (within < counts[e_c])
    pair = order[jnp.clip(gstart[e_c] + within, 0, PAIRS - 1)]
    src = jnp.where(valid, jnp.where(pair >= ROWS, pair - ROWS, pair), 0)
    spare = PAIRS + ((q // TM_E) % 2) * TM_E + (q % TM_E)
    dst = jnp.where(valid, pair, spare)
    tile_expert = e_c[::TM_E]
    n_used = (pend[-1] // TM_E).reshape(1)
    return (src.reshape(NT_E, 1, TM_E), dst.reshape(NT_E, 1, TM_E), tile_expert, n_used)


DMA_UNROLL = 8


def _row_dmas(per, make):
    def issue(r8, _):
        for u in range(DMA_UNROLL):
            make(r8 * DMA_UNROLL + u).start(priority=u % 2)
        return 0

    lax.fori_loop(0, per // DMA_UNROLL, issue, 0)


def _experts_kernel(te_ref, nu_ref, src0_ref, srcn_ref, dstp_ref, dst_ref, hn_ref, wg_ref, wu_ref,
                    wd_ref, slots_ref, xbuf, ybuf, act, gsem, ssem):
    t = pl.program_id(0)
    c = pl.program_id(1)
    n_used = nu_ref[0]
    used = t < n_used
    slot = t % 2
    n_sub = EX_CHUNK // FF_CHUNK
    per_sub = -(-TM_E // n_sub)

    def fetch_row(idx_ref, s, r):
        return pltpu.make_async_copy(hn_ref.at[pl.ds(idx_ref[0, 0, r], 1)],
                                     xbuf.at[s, pl.ds(r, 1)], gsem.at[s])

    def fetch_wait(s):
        pltpu.make_async_copy(hn_ref.at[pl.ds(0, TM_E)], xbuf.at[s], gsem.at[s]).wait()

    def send_row(idx_ref, s, r):
        return pltpu.make_async_copy(ybuf.at[s, pl.ds(r, 1)],
                                     slots_ref.at[pl.ds(idx_ref[0, 0, r], 1)], ssem.at[s])

    def send_wait(s):
        pltpu.make_async_copy(ybuf.at[s], slots_ref.at[pl.ds(0, TM_E)], ssem.at[s]).wait()

    def compute(row_dma):
        x = xbuf[slot].astype(BF16)
        for s in range(n_sub):
            sl = slice(s * FF_CHUNK, (s + 1) * FF_CHUNK)
            g = jnp.dot(x, wg_ref[0, :, sl], preferred_element_type=F32)
            u = jnp.dot(x, wu_ref[0, :, sl], preferred_element_type=F32)
            act[:, sl] = _silu_mul(g, u).astype(BF16)
            if row_dma is not None:
                for r in range(s * per_sub, min((s + 1) * per_sub, TM_E)):
                    row_dma(r).start(priority=r % 2)
        return jnp.dot(act[...], wd_ref[0], preferred_element_type=F32)

    @pl.when((t == 0) & (c == 0))
    def _():
        ybuf[0] = jnp.zeros((TM_E, D_MODEL), F32)
        spare = [pltpu.make_async_copy(ybuf.at[0], slots_ref.at[pl.ds(PAIRS + k * TM_E, TM_E)],
                                       ssem.at[0])
                 for k in range((SLOT_ROWS - PAIRS) // TM_E)]
        for cp in spare:
            cp.start()
        for cp in spare:
            cp.wait()
        _row_dmas(TM_E, lambda r: fetch_row(src0_ref, 0, r))

    @pl.when(used & (c == 0))
    def _():
        fetch_wait(slot)

        @pl.when(t >= 2)
        def _():
            send_wait(slot)

    @pl.when(used & (c == 0))
    def _():
        ybuf[slot] = compute(lambda r: fetch_row(srcn_ref, 1 - slot, r))

    @pl.when(used & (c == 1) & (t >= 1))
    def _():
        ybuf[slot] += compute(lambda r: send_row(dstp_ref, 1 - slot, r))

    @pl.when(used & (c == 1) & (t == 0))
    def _():
        ybuf[slot] += compute(None)

    @pl.when((t == n_used - 1) & (c == 1))
    def _():
        _row_dmas(TM_E, lambda r: send_row(dst_ref, slot, r))

    @pl.when((t == pl.num_programs(0) - 1) & (c == 1))
    def _():
        send_wait((n_used - 1) % 2)
        fetch_wait(n_used % 2)

        @pl.when(n_used >= 2)
        def _():
            send_wait(n_used % 2)


def _experts(tile_expert, n_used, src, dst, hn, wg, wu, wd):
    n_chunks = D_EXPERT // EX_CHUNK
    assert n_chunks == 2

    def held(t, c, nu):
        live = t < nu[0]
        return jnp.where(live, t, nu[0] - 1), jnp.where(live, c, n_chunks - 1)

    def w_in_map(t, c, te, nu):
        tt, cc = held(t, c, nu)
        return te[tt], 0, cc

    def w_out_map(t, c, te, nu):
        tt, cc = held(t, c, nu)
        return te[tt], cc, 0

    idx_block = lambda fn: pl.BlockSpec((1, 1, TM_E), fn, memory_space=pltpu.SMEM)
    grid_spec = pltpu.PrefetchScalarGridSpec(
        num_scalar_prefetch=2,
        grid=(NT_E, n_chunks),
        in_specs=[idx_block(lambda t, c, te, nu: (0, 0, 0)),
                  idx_block(lambda t, c, te, nu: (jnp.minimum(t + 1, NT_E - 1), 0, 0)),
                  idx_block(lambda t, c, te, nu: (jnp.maximum(t - 1, 0), 0, 0)),
                  idx_block(lambda t, c, te, nu: (t, 0, 0)),
                  pl.BlockSpec(memory_space=pl.ANY),
                  pl.BlockSpec((1, D_MODEL, EX_CHUNK), w_in_map),
                  pl.BlockSpec((1, D_MODEL, EX_CHUNK), w_in_map),
                  pl.BlockSpec((1, EX_CHUNK, D_MODEL), w_out_map)],
        out_specs=pl.BlockSpec(memory_space=pl.ANY),
        scratch_shapes=[pltpu.VMEM((2, TM_E, D_MODEL), F32), pltpu.VMEM((2, TM_E, D_MODEL), F32),
                        pltpu.VMEM((TM_E, EX_CHUNK), BF16),
                        pltpu.SemaphoreType.DMA((2,)), pltpu.SemaphoreType.DMA((2,))])
    return pl.pallas_call(
        _experts_kernel,
        grid_spec=grid_spec,
        out_shape=jax.ShapeDtypeStruct((SLOT_ROWS, D_MODEL), F32),
        compiler_params=_params(("arbitrary", "arbitrary")),
        name="experts",
    )(tile_expert, n_used, src, src, dst, dst, hn, wg, wu, wd)


def _combine_kernel(hmid_ref, a_ref, b_ref, rt_ref, fg_ref, yp_ref, ys_ref):
    i = pl.program_id(0)
    lane = lax.broadcasted_iota(I32, rt_ref.shape, 1)
    rt = rt_ref[...]
    g1 = jnp.sum(jnp.where(lane == 2, rt, 0.0), axis=-1, keepdims=True)
    g2 = jnp.sum(jnp.where(lane == 3, rt, 0.0), axis=-1, keepdims=True)
    y = _rms(hmid_ref[...] + g1 * a_ref[...] + g2 * b_ref[...], fg_ref[...])
    prompt = i < ROWS_P // TILE

    @pl.when(prompt & (i % SEQ_TILES != 0))
    def _():
        yp_ref[0] = y

    @pl.when(jnp.logical_not(prompt))
    def _():
        ys_ref[...] = y


def _combine(hmid, slots, route, fgain):
    n_p = ROWS_P // TILE

    def yp_map(i):
        ip = jnp.minimum(i, n_p - 1)
        return (ip // SEQ_TILES, jnp.maximum(ip % SEQ_TILES - 1, 0), 0)

    return pl.pallas_call(
        _combine_kernel,
        grid=(ROWS // TILE,),
        in_specs=[pl.BlockSpec((TILE, D_MODEL), lambda i: (i, 0)),
                  pl.BlockSpec((TILE, D_MODEL), lambda i: (i, 0)),
                  pl.BlockSpec((TILE, D_MODEL), lambda i: (i + ROWS // TILE, 0)),
                  pl.BlockSpec((TILE, LANES), lambda i: (i, 0)),
                  pl.BlockSpec((1, D_MODEL), lambda i: (0, 0))],
        out_specs=[pl.BlockSpec((1, TILE, D_MODEL), yp_map),
                   pl.BlockSpec((TILE, D_MODEL), lambda i: (jnp.maximum(i - n_p, 0), 0))],
        out_shape=[jax.ShapeDtypeStruct((BATCH, SEQ, D_MODEL), F32),
                   jax.ShapeDtypeStruct((ROWS_S, D_MODEL), F32)],
        compiler_params=_params(("arbitrary",)),
        name="combine",
    )(hmid, slots, slots, route, fgain)


def _moe(hmid, hn, route, wg, wu, wd, fgain):
    src, dst, tile_expert, n_used = _route_plan(route)
    slots = _experts(tile_expert, n_used, src, dst, hn, wg, wu, wd)
    return _combine(hmid, slots, route, fgain)


def _sample_qbd(q):
    qs = q[ROWS_P:].reshape(DEC_BATCH, DEC_SEQ, SB_HEADS, SB_HEAD_DIM)
    eye = jnp.eye(SB_HEADS, dtype=q.dtype)
    qbd = jnp.einsum('sthd,hg->shdgt', qs, eye)
    return qbd.reshape(DEC_BATCH, SB_WIDTH, SB_HEADS * DEC_SEQ)


def kernel(x_prompt, x_sample, cache_k, cache_v, state_pool, meta_tokens, norm_mix, w_in, pool_w, pool_scale, w_out, norm_ffn, ffn_w_gate, ffn_w_up, ffn_w_down, moe_router, moe_w_gate, moe_w_up, moe_w_down, final_norm):
    h = _token_rows(x_prompt, meta_tokens, x_sample.reshape(ROWS_S, D_MODEL))
    perm = _key_permutation()

    outs = {name: [] for name in ("pp", "sp")}
    y_prompt = y_sample = None
    kv_bufs = (jnp.zeros((DEPTH, ROWS, 512), F32), jnp.zeros((DEPTH, ROWS, 512), F32))
    for l in range(DEPTH):
        pv, q, k, v, kp, qt3, vt3 = _inproj(h, norm_mix[l][None], w_in[l].astype(BF16), perm,
                                            l, kv_bufs)
        kv_bufs = (k, v)

        sb_p = _attn_prompt(kp, qt3, vt3)
        sb_s = _attn_sample(cache_k[l].reshape(DEC_BATCH, PAST_LEN, SB_WIDTH),
                            cache_v[l].reshape(DEC_BATCH, PAST_LEN, SB_WIDTH),
                            k, v, _sample_qbd(q), l)

        pw = pool_w[l].astype(BF16)
        ps = pool_scale[l][None]
        pool_p = _pool(pv, pw, ps, rows=ROWS_P, tm=TILE, use_halo=True)
        pv_s = pv[ROWS_P:].reshape(DEC_BATCH, DEC_SEQ, POOL_WIDTH)
        pbuf = jnp.concatenate([jnp.zeros((DEC_BATCH, 1, POOL_WIDTH), F32), state_pool[l], pv_s],
                               axis=1).reshape(DEC_BATCH * 2 * DEC_SEQ, POOL_WIDTH)
        pool_s = _pool(pbuf, pw, ps, rows=DEC_BATCH * 2 * DEC_SEQ, tm=TM, use_halo=False)
        pool_s = pool_s.reshape(DEC_BATCH, 2 * DEC_SEQ, POOL_WIDTH)[:, DEC_SEQ:].reshape(ROWS_S, POOL_WIDTH)

        wo = w_out[l].astype(BF16)
        j = l // 2
        if l % 2 == 0:
            hmid, hn = _mix(h, pool_p, pool_s, sb_p, sb_s, wo, norm_ffn[l][None])
            h = _ffn(hmid, hn, ffn_w_gate[j].astype(BF16), ffn_w_up[j].astype(BF16),
                     ffn_w_down[j].astype(BF16))
        else:
            router = jnp.pad(moe_router[j], ((0, 0), (0, LANES - N_EXPERTS)))
            hmid, hn, route = _mix(h, pool_p, pool_s, sb_p, sb_s, wo, norm_ffn[l][None], router)
            y_prompt, y_sample = _moe(hmid, hn, route, moe_w_gate[j].astype(BF16),
                                      moe_w_up[j].astype(BF16), moe_w_down[j].astype(BF16),
                                      final_norm[None])

        outs["pp"].append(pv[:ROWS_P].reshape(BATCH, SEQ_PAD, POOL_WIDTH)[:, -POOL_STATE:])
        outs["sp"].append(pv_s[:, -POOL_STATE:])

    def prompt_rows(buf):
        rows = buf[:, :ROWS_P].reshape(DEPTH, BATCH, SEQ_PAD, SB_HEADS, SB_HEAD_DIM)
        return rows[:, :, FRONT:]

    def sample_rows(buf):
        return buf[:, ROWS_P:].reshape(DEPTH, DEC_BATCH, DEC_SEQ, SB_HEADS, SB_HEAD_DIM)

    k, v = kv_bufs
    y_sample = y_sample.reshape(DEC_BATCH, DEC_SEQ, D_MODEL)
    return (y_prompt, y_sample, prompt_rows(k), prompt_rows(v), jnp.stack(outs["pp"]),
            sample_rows(k), sample_rows(v), jnp.stack(outs["sp"]))
```

```python
import functools

import jax
import jax.numpy as jnp
from jax import lax
from jax.experimental import pallas as pl
from jax.experimental.pallas import tpu as pltpu

F32 = jnp.float32
BF16 = jnp.bfloat16
I32 = jnp.int32

D_MODEL = 1024
BATCH = 2
SEQ = 8192
DEPTH = 2
DEC_BATCH = 32
DEC_SEQ = 16
PAST_LEN = 1024
N_META = 16
POOL_WIDTH = 512
POOL_WINDOWS = (2, 4, 8, 16)
POOL_GROUP_DIM = 128
POOL_STATE = 15
SB_HEADS = 8
SB_HEAD_DIM = 64
SB_WIDTH = 512
IN_WIDTH = 2048
D_FF = 2816
N_EXPERTS = 8
D_EXPERT = 3584
EPS = 1e-6

LANES = 128
SUBLANES = 8
TILE = 256
SEG = TILE // SUBLANES
FRONT = TILE - N_META
SEQ_PAD = FRONT + N_META + SEQ
SEQ_TILES = SEQ_PAD // TILE
ROWS_P = BATCH * SEQ_PAD
ROWS_S = DEC_BATCH * DEC_SEQ
ROWS = ROWS_P + ROWS_S
TM = 512
HALO = 16
VMEM_LIMIT = 56 * 1024 * 1024

TM_E = 512
PAIRS = 2 * ROWS
P_MAX = PAIRS + N_EXPERTS * TM_E
NT_E = P_MAX // TM_E
SLOT_ROWS = PAIRS + 2 * TM_E
EX_CHUNK = 1792


def _params(sem, vmem=VMEM_LIMIT):
    return pltpu.CompilerParams(dimension_semantics=sem, vmem_limit_bytes=vmem)


def _rms(x, gain):
    ms = jnp.mean(x * x, axis=-1, keepdims=True)
    return x * lax.rsqrt(ms + EPS) * gain


def _token_rows_kernel(xp_ref, meta_ref, xs_ref, o_ref):
    i = pl.program_id(0)
    t = i % SEQ_TILES
    prompt = i < ROWS_P // TILE

    @pl.when(prompt & (t == 0))
    def _():
        o_ref[...] = jnp.concatenate([jnp.zeros((FRONT, D_MODEL), F32), meta_ref[...]], axis=0)

    @pl.when(prompt & (t > 0))
    def _():
        o_ref[...] = xp_ref[0]

    @pl.when(jnp.logical_not(prompt))
    def _():
        o_ref[...] = xs_ref[...]


def _token_rows(x_prompt, meta, x_sample):
    n_p = ROWS_P // TILE

    def xp_map(i):
        ip = jnp.minimum(i, n_p - 1)
        return ip // SEQ_TILES, jnp.maximum(ip % SEQ_TILES - 1, 0), 0

    return pl.pallas_call(
        _token_rows_kernel,
        grid=(ROWS // TILE,),
        in_specs=[pl.BlockSpec((1, TILE, D_MODEL), xp_map),
                  pl.BlockSpec((N_META, D_MODEL), lambda i: (0, 0)),
                  pl.BlockSpec((TILE, D_MODEL), lambda i: (jnp.maximum(i - n_p, 0), 0))],
        out_specs=pl.BlockSpec((TILE, D_MODEL), lambda i: (i, 0)),
        out_shape=jax.ShapeDtypeStruct((ROWS, D_MODEL), F32),
        compiler_params=_params(("arbitrary",)),
        name="token_rows",
    )(x_prompt, meta, x_sample)


def _inproj_kernel(*refs):
    x_ref, g_ref, w_ref, perm_ref = refs[:4]
    pv_ref, q_ref, k_ref, v_ref, kp_ref, qt_ref, vt_ref = refs[-7:]
    hn = _rms(x_ref[...], g_ref[...]).astype(BF16)
    pv_ref[...] = jnp.dot(hn, w_ref[:, 0:512], preferred_element_type=F32)
    q = jnp.dot(hn, w_ref[:, 512:1024], preferred_element_type=F32) * (SB_HEAD_DIM ** -0.5)
    k = jnp.dot(hn, w_ref[:, 1024:1536], preferred_element_type=F32)
    v = jnp.dot(hn, w_ref[:, 1536:2048], preferred_element_type=F32)
    q_ref[...] = q.astype(BF16)
    k_ref[...] = k
    v_ref[...] = v
    perm = perm_ref[...]
    for half in range(TM // TILE):
        sl = slice(half * TILE, (half + 1) * TILE)
        kp_ref[sl, :] = jnp.dot(perm, k[sl].astype(BF16), preferred_element_type=F32).astype(BF16)
        vp = jnp.dot(perm, v[sl].astype(BF16), preferred_element_type=F32)
        vt_ref[half] = vp.T.astype(BF16)
        qt_ref[half] = (q[sl] * 0.5).T.astype(BF16)


def _inproj(h, gain, w, perm, layer, kv_bufs):
    row = lambda i: (i, 0)
    fixed = lambda i: (0, 0)
    tiles = lambda i: (i, 0, 0)
    lrow = lambda i: (layer, i, 0)
    per = TM // TILE
    kv_block = pl.BlockSpec((None, TM, 512), lrow)
    kv_shape = jax.ShapeDtypeStruct((DEPTH, ROWS, 512), F32)
    in_specs = [pl.BlockSpec((TM, D_MODEL), row),
                pl.BlockSpec((1, D_MODEL), fixed),
                pl.BlockSpec((D_MODEL, IN_WIDTH), fixed),
                pl.BlockSpec((TILE, TILE), fixed),
                pl.BlockSpec(memory_space=pl.ANY), pl.BlockSpec(memory_space=pl.ANY)]
    args = [h, gain, w, perm, *kv_bufs]
    aliases = {4: 2, 5: 3}
    return pl.pallas_call(
        _inproj_kernel,
        grid=(ROWS // TM,),
        in_specs=in_specs,
        out_specs=[pl.BlockSpec((TM, 512), row), pl.BlockSpec((TM, 512), row),
                   kv_block, kv_block,
                   pl.BlockSpec((TM, 512), row)]
        + [pl.BlockSpec((per, SB_WIDTH, TILE), tiles)] * 2,
        out_shape=[jax.ShapeDtypeStruct((ROWS, 512), F32),
                   jax.ShapeDtypeStruct((ROWS, 512), BF16),
                   kv_shape, kv_shape,
                   jax.ShapeDtypeStruct((ROWS, 512), BF16),
                   jax.ShapeDtypeStruct((ROWS // TILE, SB_WIDTH, TILE), BF16),
                   jax.ShapeDtypeStruct((ROWS // TILE, SB_WIDTH, TILE), BF16)],
        input_output_aliases=aliases,
        compiler_params=_params(("arbitrary",)),
        name="inproj",
    )(*args)


def _key_permutation():
    r = jnp.arange(TILE)
    src = (r % SUBLANES) * SEG + r // SUBLANES
    return (src[:, None] == jnp.arange(TILE)[None, :]).astype(BF16)


def _sb_half(zh, carry, lane0, masked):
    om = 0.5 - 0.5 * jnp.tanh(zh)
    if masked:
        r = lax.broadcasted_iota(I32, (TILE, LANES), 0)
        c = lax.broadcasted_iota(I32, (TILE, LANES), 1) + lane0
        key = ((r & (SUBLANES - 1)) * SEG) + (r >> 3)
        om = jnp.where(key < c, om, 1.0)
    run = jnp.ones((SUBLANES, LANES), F32)
    parts = [None] * SEG
    for a in reversed(range(SEG)):
        nxt = run * om[a * SUBLANES:(a + 1) * SUBLANES]
        parts[a] = run - nxt
        run = nxt
    sub = lax.broadcasted_iota(I32, (SUBLANES, LANES), 0)
    y = run
    for k in (1, 2, 4):
        y = y * jnp.where(sub + k < SUBLANES, pltpu.roll(y, SUBLANES - k, 0), 1.0)
    off = carry * jnp.where(sub + 1 < SUBLANES, pltpu.roll(y, SUBLANES - 1, 0), 1.0)
    new_carry = carry * jnp.broadcast_to(y[0:1, :], (SUBLANES, LANES))
    a_t = jnp.concatenate([p * off for p in parts], axis=0).astype(BF16)
    return a_t, new_carry


def _sb_tile(z, carry, masked):
    outs = []
    carries = []
    for lh in range(TILE // LANES):
        ls = slice(lh * LANES, (lh + 1) * LANES)
        a, c = _sb_half(z[:, ls], carry[:, ls], lh * LANES, masked)
        outs.append(a)
        carries.append(c)
    return jnp.concatenate(outs, axis=1), jnp.concatenate(carries, axis=1)


def _attn_prompt_kernel(nt_ref, k_ref, qt_ref, vt_ref, o_ref, zbuf, abuf):
    half = SB_HEAD_DIM
    row = lax.broadcasted_iota(I32, (LANES, TILE), 0)

    def k_tile(j):
        return k_ref[pl.ds(pl.multiple_of(j * TILE, TILE), TILE), :]

    def q_tile(i, _):
        qt = qt_ref[i].astype(F32)
        qh = (jnp.where(row < half, qt, 0.0).astype(BF16),
              jnp.where(row >= half, qt, 0.0).astype(BF16))
        ones = jnp.ones((SUBLANES, TILE), F32)

        kd = k_tile(i)
        carries = []
        for h in range(2):
            z = jnp.dot(kd, qh[h], preferred_element_type=F32)
            a, c = _sb_tile(z, ones, True)
            abuf[0, h] = a
            abuf[1, h] = jnp.zeros((TILE, TILE), BF16)
            carries.append(c)
        k0 = k_tile(jnp.maximum(i - 1, 0))
        k1 = k_tile(jnp.maximum(i - 2, 0))
        for h in range(2):
            zbuf[0, h] = jnp.dot(k0, qh[h], preferred_element_type=F32)
            zbuf[1, h] = jnp.dot(k1, qh[h], preferred_element_type=F32)

        def weighted_values(acc, v0, v1):
            out = []
            for h in range(2):
                rows = slice(h * half, (h + 1) * half)
                out.append(acc[h]
                           + jnp.dot(v0[rows], abuf[0, h], preferred_element_type=F32)
                           + jnp.dot(v1[rows], abuf[1, h], preferred_element_type=F32))
            return out

        def trip(st):
            t, _, c0, c1, acc0, acc1, p0, p1 = st
            j0 = i - 1 - 2 * t
            j1 = jnp.maximum(j0 - 1, 0)
            acc = weighted_values((acc0, acc1), vt_ref[p0], vt_ref[p1])
            cs = [c0, c1]
            for h in range(2):
                for s in range(2):
                    a, cs[h] = _sb_tile(zbuf[s, h], cs[h], False)
                    abuf[s, h] = a
            n0 = k_tile(jnp.maximum(j0 - 2, 0))
            n1 = k_tile(jnp.maximum(j0 - 3, 0))
            for h in range(2):
                zbuf[0, h] = jnp.dot(n0, qh[h], preferred_element_type=F32)
                zbuf[1, h] = jnp.dot(n1, qh[h], preferred_element_type=F32)
            alive = jnp.maximum(jnp.max(cs[0]), jnp.max(cs[1])) > 0.0
            return t + 1, alive, cs[0], cs[1], acc[0], acc[1], j0, j1

        zeros = jnp.zeros((half, TILE), F32)
        trips = (i + 1) // 2
        st = lax.while_loop(lambda s: (s[0] < trips) & s[1], trip,
                            (jnp.int32(0), jnp.bool_(True), carries[0], carries[1],
                             zeros, zeros, i, i))
        keep1 = (i == 0) | ((i & 1) == 0) | (st[0] < trips)
        v1 = jnp.where(keep1, vt_ref[st[7]].astype(F32), 0.0).astype(BF16)
        acc = weighted_values((st[4], st[5]), vt_ref[st[6]], v1)
        out_t = jnp.concatenate(acc, axis=0)
        o_ref[pl.ds(pl.multiple_of(i * TILE, TILE), TILE), :] = out_t.T.astype(BF16)
        return 0

    lax.fori_loop(0, nt_ref[0], q_tile, 0)


def _attn_prompt(kp, qt3, vt3, *, batch=BATCH, n_tiles=SEQ_TILES):
    rows = n_tiles * TILE
    return pl.pallas_call(
        _attn_prompt_kernel,
        grid=(batch, SB_HEADS // 2),
        in_specs=[pl.BlockSpec(memory_space=pltpu.SMEM),
                  pl.BlockSpec((rows, LANES), lambda p, g: (p, g)),
                  pl.BlockSpec((n_tiles, LANES, TILE), lambda p, g: (p, g, 0)),
                  pl.BlockSpec((n_tiles, LANES, TILE), lambda p, g: (p, g, 0))],
        out_specs=pl.BlockSpec((rows, LANES), lambda p, g: (p, g)),
        out_shape=jax.ShapeDtypeStruct((batch * rows, SB_WIDTH), BF16),
        scratch_shapes=[pltpu.VMEM((2, 2, TILE, TILE), F32),
                        pltpu.VMEM((2, 2, TILE, TILE), BF16)],
        compiler_params=_params(("arbitrary", "arbitrary")),
        name="attn_prompt",
    )(jnp.full((1,), n_tiles, I32), kp, qt3, vt3)


def _rev_excl_cumprod(om, scr):
    n = om.shape[0]
    sub = lax.broadcasted_iota(I32, (n, LANES), 0) & (SUBLANES - 1)
    scr[n:n + SUBLANES, :] = jnp.ones((SUBLANES, LANES), F32)
    y = om
    for k in (1, 2, 4):
        scr[0:n, :] = y
        y = y * jnp.where(sub + k < SUBLANES, scr[k:n + k, :], 1.0)
    scr[0:n, :] = y
    ex = jnp.where(sub + 1 < SUBLANES, scr[1:n + 1, :], 1.0)
    run = jnp.ones((SUBLANES, LANES), F32)
    out = [None] * (n // SUBLANES)
    for a in reversed(range(n // SUBLANES)):
        sl = slice(a * SUBLANES, (a + 1) * SUBLANES)
        out[a] = ex[sl] * run
        run = run * jnp.broadcast_to(y[a * SUBLANES:a * SUBLANES + 1, :], (SUBLANES, LANES))
    return jnp.concatenate(out, axis=0)


def _attn_sample_kernel(ck_ref, cv_ref, kn_ref, vn_ref, qbd_ref, o_ref, scr):
    qbd = qbd_ref[0]
    pad = jnp.zeros((LANES - DEC_SEQ, SB_WIDTH), F32)
    kn = jnp.concatenate([kn_ref[...], pad], axis=0).astype(BF16)
    vn = jnp.concatenate([vn_ref[...], pad], axis=0).astype(BF16)
    zc = jnp.dot(ck_ref[0].astype(BF16), qbd, preferred_element_type=F32)
    zn = jnp.dot(kn, qbd, preferred_element_type=F32)
    z = jnp.concatenate([zc, zn], axis=0)
    n = PAST_LEN + LANES
    r = lax.broadcasted_iota(I32, (n, LANES), 0)
    c = lax.broadcasted_iota(I32, (n, LANES), 1)
    valid = (r < PAST_LEN) | ((r - PAST_LEN) < (c & (DEC_SEQ - 1)))
    e = jnp.exp(z)
    om = 1.0 / (1.0 + e)
    beta = jnp.where(valid, 1.0 - om, 0.0)
    om = jnp.where(valid, om, 1.0)
    a_t = beta * _rev_excl_cumprod(om, scr)
    ac = a_t[0:PAST_LEN].T.astype(BF16)
    an = a_t[PAST_LEN:n].T.astype(BF16)
    p = jnp.dot(ac, cv_ref[0].astype(BF16), preferred_element_type=F32)
    p = p + jnp.dot(an, vn, preferred_element_type=F32)
    pr = lax.broadcasted_iota(I32, (LANES, SB_WIDTH), 0)
    pc = lax.broadcasted_iota(I32, (LANES, SB_WIDTH), 1)
    p = jnp.where((pr >> 4) == (pc >> 6), p, 0.0)
    out = p[0:DEC_SEQ]
    for h in range(1, SB_HEADS):
        out = out + p[h * DEC_SEQ:(h + 1) * DEC_SEQ]
    o_ref[...] = out.astype(BF16)


def _attn_sample(ck, cv, k, v, qbd, layer):
    new = lambda s: (layer, ROWS_P // DEC_SEQ + s, 0)
    per = lambda s: (s, 0, 0)
    cached = lambda s: (layer * DEC_BATCH + s, 0, 0)
    return pl.pallas_call(
        _attn_sample_kernel,
        grid=(DEC_BATCH,),
        in_specs=[pl.BlockSpec((1, PAST_LEN, SB_WIDTH), cached),
                  pl.BlockSpec((1, PAST_LEN, SB_WIDTH), cached),
                  pl.BlockSpec((None, DEC_SEQ, SB_WIDTH), new),
                  pl.BlockSpec((None, DEC_SEQ, SB_WIDTH), new),
                  pl.BlockSpec((1, SB_WIDTH, LANES), per)],
        out_specs=pl.BlockSpec((DEC_SEQ, SB_WIDTH), lambda s: (s, 0)),
        out_shape=jax.ShapeDtypeStruct((ROWS_S, SB_WIDTH), BF16),
        scratch_shapes=[pltpu.VMEM((PAST_LEN + LANES + SUBLANES, LANES), F32)],
        compiler_params=_params(("arbitrary",)),
        name="attn_sample",
    )(ck, cv, k, v, qbd)


def _pool_kernel(*refs, tm, use_halo):
    if use_halo:
        pv_ref, halo_ref, w_ref, sc_ref, o_ref, s1, s2, s4, s8 = refs
    else:
        pv_ref, w_ref, sc_ref, o_ref, s1, s2, s4, s8 = refs
    i = pl.program_id(0)
    off = SUBLANES
    n = HALO + tm
    x = pv_ref[...]
    zero8 = jnp.zeros((off, POOL_WIDTH), F32)
    if use_halo:
        first = (i % (SEQ_PAD // tm)) == 0
        halo = jnp.where(first, 0.0, halo_ref[...])
    else:
        halo = jnp.zeros((HALO, POOL_WIDTH), F32)
    s1[0:off, :] = zero8
    s1[off:off + HALO, :] = halo
    s1[off + HALO:off + n, :] = x
    g = POOL_GROUP_DIM
    t2 = s1[off:off + n, :] + s1[off - 1:off - 1 + n, :]
    s2[0:off, :] = zero8[:, g:]
    s2[off:off + n, :] = t2[:, g:]
    t4 = t2[:, g:] + s2[off - 2:off - 2 + n, :]
    s4[0:off, :] = zero8[:, 2 * g:]
    s4[off:off + n, :] = t4[:, g:]
    t8 = t4[:, g:] + s4[off - 4:off - 4 + n, :]
    s8[0:off, :] = zero8[:, 3 * g:]
    s8[off:off + n, :] = t8[:, g:]
    t16 = t8[:, g:] + s8[off - 8:off - 8 + n, :]
    sums = (t2[HALO:, 0:g], t4[HALO:, 0:g], t8[HALO:, 0:g], t16[HALO:, :])
    if use_halo:
        idx = (i % (SEQ_PAD // tm)) * tm - FRONT + lax.broadcasted_iota(I32, (tm, g), 0)
    for gi, win in enumerate(POOL_WINDOWS):
        if use_halo:
            cnt = jnp.clip(idx + 1, 1, win).astype(F32)
        else:
            cnt = jnp.full((tm, g), float(win), F32)
        d = sums[gi] / cnt - x[:, gi * g:(gi + 1) * g]
        y = jnp.dot(d.astype(BF16), w_ref[gi], preferred_element_type=F32)
        o_ref[:, gi * g:(gi + 1) * g] = (y * sc_ref[:, gi * g:(gi + 1) * g]).astype(BF16)


def _pool(pv, w, scale, *, rows, tm, use_halo):
    g = POOL_GROUP_DIM
    in_specs = [pl.BlockSpec((tm, POOL_WIDTH), lambda i: (i, 0))]
    args = [pv]
    if use_halo:
        in_specs.append(pl.BlockSpec((HALO, POOL_WIDTH),
                                     lambda i: (jnp.maximum(i * (tm // HALO) - 1, 0), 0)))
        args.append(pv)
    in_specs += [pl.BlockSpec((4, g, g), lambda i: (0, 0, 0)),
                 pl.BlockSpec((1, POOL_WIDTH), lambda i: (0, 0))]
    args += [w, scale]
    n = SUBLANES + HALO + tm
    return pl.pallas_call(
        functools.partial(_pool_kernel, tm=tm, use_halo=use_halo),
        grid=(rows // tm,),
        in_specs=in_specs,
        out_specs=pl.BlockSpec((tm, POOL_WIDTH), lambda i: (i, 0)),
        out_shape=jax.ShapeDtypeStruct((rows, POOL_WIDTH), BF16),
        scratch_shapes=[pltpu.VMEM((n, 4 * g), F32), pltpu.VMEM((n, 3 * g), F32),
                        pltpu.VMEM((n, 2 * g), F32), pltpu.VMEM((n, g), F32)],
        compiler_params=_params(("arbitrary",)),
        name="pool_prompt" if use_halo else "pool_sample",
    )(*args)


def _mix_kernel(*refs, route):
    if route:
        (h_ref, pp_ref, ps_ref, sp_ref, ss_ref, w_ref, g_ref, r_ref,
         hmid_ref, hn_ref, rt_ref) = refs
    else:
        h_ref, pp_ref, ps_ref, sp_ref, ss_ref, w_ref, g_ref, hmid_ref, hn_ref = refs
    prompt = pl.program_id(0) < ROWS_P // TM
    pool = jnp.where(prompt, pp_ref[...], ps_ref[...])
    sb = jnp.where(prompt, sp_ref[...], ss_ref[...])
    hmid = (h_ref[...]
            + jnp.dot(pool, w_ref[0:POOL_WIDTH, :], preferred_element_type=F32)
            + jnp.dot(sb, w_ref[POOL_WIDTH:, :], preferred_element_type=F32))
    hmid_ref[...] = hmid
    hn = _rms(hmid, g_ref[...])
    hn_ref[...] = hn.astype(hn_ref.dtype)
    if route:
        tm = hn.shape[0]
        h_hi = hn.astype(BF16)
        h_lo = (hn - h_hi.astype(F32)).astype(BF16)
        r = r_ref[...]
        r_hi = r.astype(BF16)
        r_lo = (r - r_hi.astype(F32)).astype(BF16)
        logits = (jnp.dot(h_hi, r_hi, preferred_element_type=F32)
                  + jnp.dot(h_hi, r_lo, preferred_element_type=F32)
                  + jnp.dot(h_lo, r_hi, preferred_element_type=F32))
        lane = lax.broadcasted_iota(I32, (tm, LANES), 1)
        neg = jnp.float32(-jnp.inf)
        logits = jnp.where(lane < N_EXPERTS, logits, neg)
        m1 = jnp.max(logits, axis=-1, keepdims=True)
        i1 = jnp.min(jnp.where(logits == m1, lane, LANES), axis=-1, keepdims=True)
        rest = jnp.where(lane == i1, neg, logits)
        m2 = jnp.max(rest, axis=-1, keepdims=True)
        i2 = jnp.min(jnp.where(rest == m2, lane, LANES), axis=-1, keepdims=True)
        t = jnp.exp(m2 - m1)
        g1 = 1.0 / (1.0 + t)
        g2 = t / (1.0 + t)
        rt_ref[...] = (jnp.where(lane == 0, i1.astype(F32), 0.0)
                       + jnp.where(lane == 1, i2.astype(F32), 0.0)
                       + jnp.where(lane == 2, g1, 0.0) + jnp.where(lane == 3, g2, 0.0))


def _mix(h, pool_p, pool_s, sb_p, sb_s, w, gain, router=None):
    row = lambda i: (i, 0)
    fixed = lambda i: (0, 0)
    prow = lambda i: (jnp.minimum(i, ROWS_P // TM - 1), 0)
    route = router is not None
    in_specs = [pl.BlockSpec((TM, D_MODEL), row),
                pl.BlockSpec((TM, POOL_WIDTH), prow),
                pl.BlockSpec((TM, POOL_WIDTH), fixed),
                pl.BlockSpec((TM, SB_WIDTH), prow),
                pl.BlockSpec((TM, SB_WIDTH), fixed),
                pl.BlockSpec((D_MODEL, D_MODEL), fixed),
                pl.BlockSpec((1, D_MODEL), fixed)]
    out_specs = [pl.BlockSpec((TM, D_MODEL), row), pl.BlockSpec((TM, D_MODEL), row)]
    out_shape = [jax.ShapeDtypeStruct((ROWS, D_MODEL), F32),
                 jax.ShapeDtypeStruct((ROWS, D_MODEL), F32 if route else BF16)]
    args = [h, pool_p, pool_s, sb_p, sb_s, w, gain]
    if route:
        in_specs.append(pl.BlockSpec((D_MODEL, LANES), fixed))
        out_specs.append(pl.BlockSpec((TM, LANES), row))
        out_shape.append(jax.ShapeDtypeStruct((ROWS, LANES), F32))
        args.append(router)
    return pl.pallas_call(
        functools.partial(_mix_kernel, route=route),
        grid=(ROWS // TM,),
        in_specs=in_specs, out_specs=out_specs, out_shape=out_shape,
        compiler_params=_params(("arbitrary",)),
        name="mix_route" if route else "mix",
    )(*args)


def _silu_mul(g, u):
    return g * (1.0 / (1.0 + jnp.exp(-g))) * u


FF_CHUNK = 256


def _ffn_kernel(hmid_ref, hn_ref, wg_ref, wu_ref, wd_ref, o_ref, act):
    hn = hn_ref[...]
    for c in range(D_FF // FF_CHUNK):
        sl = slice(c * FF_CHUNK, (c + 1) * FF_CHUNK)
        g = jnp.dot(hn, wg_ref[:, sl], preferred_element_type=F32)
        u = jnp.dot(hn, wu_ref[:, sl], preferred_element_type=F32)
        act[:, sl] = _silu_mul(g, u).astype(BF16)
    o_ref[...] = hmid_ref[...] + jnp.dot(act[...], wd_ref[...], preferred_element_type=F32)


def _ffn(hmid, hn, wg, wu, wd):
    row = lambda i: (i, 0)
    fixed = lambda i: (0, 0)
    once = pl.Buffered(1)
    return pl.pallas_call(
        _ffn_kernel,
        grid=(ROWS // TM,),
        in_specs=[pl.BlockSpec((TM, D_MODEL), row),
                  pl.BlockSpec((TM, D_MODEL), row),
                  pl.BlockSpec((D_MODEL, D_FF), fixed, pipeline_mode=once),
                  pl.BlockSpec((D_MODEL, D_FF), fixed, pipeline_mode=once),
                  pl.BlockSpec((D_FF, D_MODEL), fixed, pipeline_mode=once)],
        out_specs=pl.BlockSpec((TM, D_MODEL), row),
        out_shape=jax.ShapeDtypeStruct((ROWS, D_MODEL), F32),
        scratch_shapes=[pltpu.VMEM((TM, D_FF), BF16)],
        compiler_params=_params(("arbitrary",)),
        name="ffn",
    )(hmid, hn, wg, wu, wd)


def _route_plan(route):
    e_flat = jnp.concatenate([route[:, 0], route[:, 1]]).astype(I32)
    order = jnp.argsort(e_flat).astype(I32)
    ex = jnp.arange(N_EXPERTS, dtype=I32)
    counts = jnp.sum((e_flat[:, None] == ex[None, :]).astype(I32), axis=0)
    gstart = jnp.cumsum(counts) - counts
    padded = ((counts + TM_E - 1) // TM_E) * TM_E
    pend = jnp.cumsum(padded)
    pstart = pend - padded
    q = jnp.arange(P_MAX, dtype=I32)
    e_q = jnp.sum((q[:, None] >= pend[None, :]).astype(I32), axis=1)
    e_c = jnp.minimum(e_q, N_EXPERTS - 1)
    within = q - pstart[e_c]
    valid = (e_q < N_EXPERTS) & (within < counts[e_c])
    pair = order[jnp.clip(gstart[e_c] + within, 0, PAIRS - 1)]
    src = jnp.where(valid, jnp.where(pair >= ROWS, pair - ROWS, pair), 0)
    spare = PAIRS + ((q // TM_E) % 2) * TM_E + (q % TM_E)
    dst = jnp.where(valid, pair, spare)
    tile_expert = e_c[::TM_E]
    n_used = (pend[-1] // TM_E).reshape(1)
    return (src.reshape(NT_E, 1, TM_E), dst.reshape(NT_E, 1, TM_E), tile_expert, n_used)


DMA_UNROLL = 8


def _row_dmas(per, make):
    def issue(r8, _):
        for u in range(DMA_UNROLL):
            make(r8 * DMA_UNROLL + u).start(priority=u % 2)
        return 0

    lax.fori_loop(0, per // DMA_UNROLL, issue, 0)


def _experts_kernel(te_ref, nu_ref, src0_ref, srcn_ref, dstp_ref, dst_ref, hn_ref, wg_ref, wu_ref,
                    wd_ref, slots_ref, xbuf, ybuf, act, gsem, ssem):
    t = pl.program_id(0)
    c = pl.program_id(1)
    n_used = nu_ref[0]
    used = t < n_used
    slot = t % 2
    n_sub = EX_CHUNK // FF_CHUNK
    per_sub = -(-TM_E // n_sub)

    def fetch_row(idx_ref, s, r):
        return pltpu.make_async_copy(hn_ref.at[pl.ds(idx_ref[0, 0, r], 1)],
                                     xbuf.at[s, pl.ds(r, 1)], gsem.at[s])

    def fetch_wait(s):
        pltpu.make_async_copy(hn_ref.at[pl.ds(0, TM_E)], xbuf.at[s], gsem.at[s]).wait()

    def send_row(idx_ref, s, r):
        return pltpu.make_async_copy(ybuf.at[s, pl.ds(r, 1)],
                                     slots_ref.at[pl.ds(idx_ref[0, 0, r], 1)], ssem.at[s])

    def send_wait(s):
        pltpu.make_async_copy(ybuf.at[s], slots_ref.at[pl.ds(0, TM_E)], ssem.at[s]).wait()

    def compute(row_dma):
        x = xbuf[slot].astype(BF16)
        for s in range(n_sub):
            sl = slice(s * FF_CHUNK, (s + 1) * FF_CHUNK)
            g = jnp.dot(x, wg_ref[0, :, sl], preferred_element_type=F32)
            u = jnp.dot(x, wu_ref[0, :, sl], preferred_element_type=F32)
            act[:, sl] = _silu_mul(g, u).astype(BF16)
            if row_dma is not None:
                for r in range(s * per_sub, min((s + 1) * per_sub, TM_E)):
                    row_dma(r).start(priority=r % 2)
        return jnp.dot(act[...], wd_ref[0], preferred_element_type=F32)

    @pl.when((t == 0) & (c == 0))
    def _():
        ybuf[0] = jnp.zeros((TM_E, D_MODEL), F32)
        spare = [pltpu.make_async_copy(ybuf.at[0], slots_ref.at[pl.ds(PAIRS + k * TM_E, TM_E)],
                                       ssem.at[0])
                 for k in range((SLOT_ROWS - PAIRS) // TM_E)]
        for cp in spare:
            cp.start()
        for cp in spare:
            cp.wait()
        _row_dmas(TM_E, lambda r: fetch_row(src0_ref, 0, r))

    @pl.when(used & (c == 0))
    def _():
        fetch_wait(slot)

        @pl.when(t >= 2)
        def _():
            send_wait(slot)

    @pl.when(used & (c == 0))
    def _():
        ybuf[slot] = compute(lambda r: fetch_row(srcn_ref, 1 - slot, r))

    @pl.when(used & (c == 1) & (t >= 1))
    def _():
        ybuf[slot] += compute(lambda r: send_row(dstp_ref, 1 - slot, r))

    @pl.when(used & (c == 1) & (t == 0))
    def _():
        ybuf[slot] += compute(None)

    @pl.when((t == n_used - 1) & (c == 1))
    def _():
        _row_dmas(TM_E, lambda r: send_row(dst_ref, slot, r))

    @pl.when((t == pl.num_programs(0) - 1) & (c == 1))
    def _():
        send_wait((n_used - 1) % 2)
        fetch_wait(n_used % 2)

        @pl.when(n_used >= 2)
        def _():
            send_wait(n_used % 2)


def _experts(tile_expert, n_used, src, dst, hn, wg, wu, wd):
    n_chunks = D_EXPERT // EX_CHUNK
    assert n_chunks == 2

    def held(t, c, nu):
        live = t < nu[0]
        return jnp.where(live, t, nu[0] - 1), jnp.where(live, c, n_chunks - 1)

    def w_in_map(t, c, te, nu):
        tt, cc = held(t, c, nu)
        return te[tt], 0, cc

    def w_out_map(t, c, te, nu):
        tt, cc = held(t, c, nu)
        return te[tt], cc, 0

    idx_block = lambda fn: pl.BlockSpec((1, 1, TM_E), fn, memory_space=pltpu.SMEM)
    grid_spec = pltpu.PrefetchScalarGridSpec(
        num_scalar_prefetch=2,
        grid=(NT_E, n_chunks),
        in_specs=[idx_block(lambda t, c, te, nu: (0, 0, 0)),
                  idx_block(lambda t, c, te, nu: (jnp.minimum(t + 1, NT_E - 1), 0, 0)),
                  idx_block(lambda t, c, te, nu: (jnp.maximum(t - 1, 0), 0, 0)),
                  idx_block(lambda t, c, te, nu: (t, 0, 0)),
                  pl.BlockSpec(memory_space=pl.ANY),
                  pl.BlockSpec((1, D_MODEL, EX_CHUNK), w_in_map),
                  pl.BlockSpec((1, D_MODEL, EX_CHUNK), w_in_map),
                  pl.BlockSpec((1, EX_CHUNK, D_MODEL), w_out_map)],
        out_specs=pl.BlockSpec(memory_space=pl.ANY),
        scratch_shapes=[pltpu.VMEM((2, TM_E, D_MODEL), F32), pltpu.VMEM((2, TM_E, D_MODEL), F32),
                        pltpu.VMEM((TM_E, EX_CHUNK), BF16),
                        pltpu.SemaphoreType.DMA((2,)), pltpu.SemaphoreType.DMA((2,))])
    return pl.pallas_call(
        _experts_kernel,
        grid_spec=grid_spec,
        out_shape=jax.ShapeDtypeStruct((SLOT_ROWS, D_MODEL), F32),
        compiler_params=_params(("arbitrary", "arbitrary")),
        name="experts",
    )(tile_expert, n_used, src, src, dst, dst, hn, wg, wu, wd)


def _combine_kernel(hmid_ref, a_ref, b_ref, rt_ref, fg_ref, yp_ref, ys_ref):
    i = pl.program_id(0)
    lane = lax.broadcasted_iota(I32, rt_ref.shape, 1)
    rt = rt_ref[...]
    g1 = jnp.sum(jnp.where(lane == 2, rt, 0.0), axis=-1, keepdims=True)
    g2 = jnp.sum(jnp.where(lane == 3, rt, 0.0), axis=-1, keepdims=True)
    y = _rms(hmid_ref[...] + g1 * a_ref[...] + g2 * b_ref[...], fg_ref[...])
    prompt = i < ROWS_P // TILE

    @pl.when(prompt & (i % SEQ_TILES != 0))
    def _():
        yp_ref[0] = y

    @pl.when(jnp.logical_not(prompt))
    def _():
        ys_ref[...] = y


def _combine(hmid, slots, route, fgain):
    n_p = ROWS_P // TILE

    def yp_map(i):
        ip = jnp.minimum(i, n_p - 1)
        return (ip // SEQ_TILES, jnp.maximum(ip % SEQ_TILES - 1, 0), 0)

    return pl.pallas_call(
        _combine_kernel,
        grid=(ROWS // TILE,),
        in_specs=[pl.BlockSpec((TILE, D_MODEL), lambda i: (i, 0)),
                  pl.BlockSpec((TILE, D_MODEL), lambda i: (i, 0)),
                  pl.BlockSpec((TILE, D_MODEL), lambda i: (i + ROWS // TILE, 0)),
                  pl.BlockSpec((TILE, LANES), lambda i: (i, 0)),
                  pl.BlockSpec((1, D_MODEL), lambda i: (0, 0))],
        out_specs=[pl.BlockSpec((1, TILE, D_MODEL), yp_map),
                   pl.BlockSpec((TILE, D_MODEL), lambda i: (jnp.maximum(i - n_p, 0), 0))],
        out_shape=[jax.ShapeDtypeStruct((BATCH, SEQ, D_MODEL), F32),
                   jax.ShapeDtypeStruct((ROWS_S, D_MODEL), F32)],
        compiler_params=_params(("arbitrary",)),
        name="combine",
    )(hmid, slots, slots, route, fgain)


def _moe(hmid, hn, route, wg, wu, wd, fgain):
    src, dst, tile_expert, n_used = _route_plan(route)
    slots = _experts(tile_expert, n_used, src, dst, hn, wg, wu, wd)
    return _combine(hmid, slots, route, fgain)


def _sample_qbd(q):
    qs = q[ROWS_P:].reshape(DEC_BATCH, DEC_SEQ, SB_HEADS, SB_HEAD_DIM)
    eye = jnp.eye(SB_HEADS, dtype=q.dtype)
    qbd = jnp.einsum('sthd,hg->shdgt', qs, eye)
    return qbd.reshape(DEC_BATCH, SB_WIDTH, SB_HEADS * DEC_SEQ)


def kernel(x_prompt, x_sample, cache_k, cache_v, state_pool, meta_tokens, norm_mix, w_in, pool_w, pool_scale, w_out, norm_ffn, ffn_w_gate, ffn_w_up, ffn_w_down, moe_router, moe_w_gate, moe_w_up, moe_w_down, final_norm):
    h = _token_rows(x_prompt, meta_tokens, x_sample.reshape(ROWS_S, D_MODEL))
    perm = _key_permutation()
    ck_all = cache_k.reshape(DEPTH * DEC_BATCH, PAST_LEN, SB_WIDTH)
    cv_all = cache_v.reshape(DEPTH * DEC_BATCH, PAST_LEN, SB_WIDTH)

    outs = {name: [] for name in ("pp", "sp")}
    y_prompt = y_sample = None
    kv_bufs = (jnp.zeros((DEPTH, ROWS, 512), F32), jnp.zeros((DEPTH, ROWS, 512), F32))
    for l in range(DEPTH):
        pv, q, k, v, kp, qt3, vt3 = _inproj(h, norm_mix[l][None], w_in[l].astype(BF16), perm,
                                            l, kv_bufs)
        kv_bufs = (k, v)

        sb_p = _attn_prompt(kp, qt3, vt3)
        sb_s = _attn_sample(ck_all, cv_all, k, v, _sample_qbd(q), l)

        pw = pool_w[l].astype(BF16)
        ps = pool_scale[l][None]
        pool_p = _pool(pv, pw, ps, rows=ROWS_P, tm=TILE, use_halo=True)
        pv_s = pv[ROWS_P:].reshape(DEC_BATCH, DEC_SEQ, POOL_WIDTH)
        pbuf = jnp.concatenate([jnp.zeros((DEC_BATCH, 1, POOL_WIDTH), F32), state_pool[l], pv_s],
                               axis=1).reshape(DEC_BATCH * 2 * DEC_SEQ, POOL_WIDTH)
        pool_s = _pool(pbuf, pw, ps, rows=DEC_BATCH * 2 * DEC_SEQ, tm=TM, use_halo=False)
        pool_s = pool_s.reshape(DEC_BATCH, 2 * DEC_SEQ, POOL_WIDTH)[:, DEC_SEQ:].reshape(ROWS_S, POOL_WIDTH)

        wo = w_out[l].astype(BF16)
        j = l // 2
        if l % 2 == 0:
            hmid, hn = _mix(h, pool_p, pool_s, sb_p, sb_s, wo, norm_ffn[l][None])
            h = _ffn(hmid, hn, ffn_w_gate[j].astype(BF16), ffn_w_up[j].astype(BF16),
                     ffn_w_down[j].astype(BF16))
        else:
            router = jnp.pad(moe_router[j], ((0, 0), (0, LANES - N_EXPERTS)))
            hmid, hn, route = _mix(h, pool_p, pool_s, sb_p, sb_s, wo, norm_ffn[l][None], router)
            y_prompt, y_sample = _moe(hmid, hn, route, moe_w_gate[j].astype(BF16),
                                      moe_w_up[j].astype(BF16), moe_w_down[j].astype(BF16),
                                      final_norm[None])

        outs["pp"].append(pv[:ROWS_P].reshape(BATCH, SEQ_PAD, POOL_WIDTH)[:, -POOL_STATE:])
        outs["sp"].append(pv_s[:, -POOL_STATE:])

    def prompt_rows(buf):
        rows = buf[:, :ROWS_P].reshape(DEPTH, BATCH, SEQ_PAD, SB_HEADS, SB_HEAD_DIM)
        return rows[:, :, FRONT:]

    def sample_rows(buf):
        return buf[:, ROWS_P:].reshape(DEPTH, DEC_BATCH, DEC_SEQ, SB_HEADS, SB_HEAD_DIM)

    k, v = kv_bufs
    y_sample = y_sample.reshape(DEC_BATCH, DEC_SEQ, D_MODEL)
    return (y_prompt, y_sample, prompt_rows(k), prompt_rows(v), jnp.stack(outs["pp"]),
            sample_rows(k), sample_rows(v), jnp.stack(outs["sp"]))
```

```python
import functools

import jax
import jax.numpy as jnp
from jax import lax
from jax.experimental import pallas as pl
from jax.experimental.pallas import tpu as pltpu

F32 = jnp.float32
BF16 = jnp.bfloat16
I32 = jnp.int32

D_MODEL = 1024
BATCH = 2
SEQ = 8192
DEPTH = 2
DEC_BATCH = 32
DEC_SEQ = 16
PAST_LEN = 1024
N_META = 16
POOL_WIDTH = 512
POOL_WINDOWS = (2, 4, 8, 16)
POOL_GROUP_DIM = 128
POOL_STATE = 15
SB_HEADS = 8
SB_HEAD_DIM = 64
SB_WIDTH = 512
IN_WIDTH = 2048
D_FF = 2816
N_EXPERTS = 8
D_EXPERT = 3584
EPS = 1e-6

LANES = 128
SUBLANES = 8
TILE = 256
SEG = TILE // SUBLANES
FRONT = TILE - N_META
SEQ_PAD = FRONT + N_META + SEQ
SEQ_TILES = SEQ_PAD // TILE
ROWS_P = BATCH * SEQ_PAD
ROWS_S = DEC_BATCH * DEC_SEQ
ROWS = ROWS_P + ROWS_S
TM = 512
HALO = 16
VMEM_LIMIT = 56 * 1024 * 1024

TM_E = 512
PAIRS = 2 * ROWS
P_MAX = PAIRS + N_EXPERTS * TM_E
NT_E = P_MAX // TM_E
SLOT_ROWS = PAIRS + 2 * TM_E
EX_CHUNK = 1792


def _params(sem, vmem=VMEM_LIMIT):
    return pltpu.CompilerParams(dimension_semantics=sem, vmem_limit_bytes=vmem)


def _rms(x, gain):
    ms = jnp.mean(x * x, axis=-1, keepdims=True)
    return x * lax.rsqrt(ms + EPS) * gain


def _token_rows_kernel(xp_ref, meta_ref, xs_ref, o_ref):
    i = pl.program_id(0)
    t = i % SEQ_TILES
    prompt = i < ROWS_P // TILE

    @pl.when(prompt & (t == 0))
    def _():
        o_ref[...] = jnp.concatenate([jnp.zeros((FRONT, D_MODEL), F32), meta_ref[...]], axis=0)

    @pl.when(prompt & (t > 0))
    def _():
        o_ref[...] = xp_ref[0]

    @pl.when(jnp.logical_not(prompt))
    def _():
        o_ref[...] = xs_ref[...]


def _token_rows(x_prompt, meta, x_sample):
    n_p = ROWS_P // TILE

    def xp_map(i):
        ip = jnp.minimum(i, n_p - 1)
        return ip // SEQ_TILES, jnp.maximum(ip % SEQ_TILES - 1, 0), 0

    return pl.pallas_call(
        _token_rows_kernel,
        grid=(ROWS // TILE,),
        in_specs=[pl.BlockSpec((1, TILE, D_MODEL), xp_map),
                  pl.BlockSpec((N_META, D_MODEL), lambda i: (0, 0)),
                  pl.BlockSpec((TILE, D_MODEL), lambda i: (jnp.maximum(i - n_p, 0), 0))],
        out_specs=pl.BlockSpec((TILE, D_MODEL), lambda i: (i, 0)),
        out_shape=jax.ShapeDtypeStruct((ROWS, D_MODEL), F32),
        compiler_params=_params(("arbitrary",)),
        name="token_rows",
    )(x_prompt, meta, x_sample)


def _inproj_kernel(*refs):
    x_ref, g_ref, w_ref, perm_ref = refs[:4]
    pv_ref, q_ref, k_ref, v_ref, kp_ref, qt_ref, vt_ref = refs[-7:]
    hn = _rms(x_ref[...], g_ref[...]).astype(BF16)
    pv_ref[...] = jnp.dot(hn, w_ref[:, 0:512], preferred_element_type=F32)
    q = jnp.dot(hn, w_ref[:, 512:1024], preferred_element_type=F32) * (SB_HEAD_DIM ** -0.5)
    k = jnp.dot(hn, w_ref[:, 1024:1536], preferred_element_type=F32)
    v = jnp.dot(hn, w_ref[:, 1536:2048], preferred_element_type=F32)
    q_ref[...] = q.astype(BF16)
    k_ref[...] = k
    v_ref[...] = v
    perm = perm_ref[...]
    for half in range(TM // TILE):
        sl = slice(half * TILE, (half + 1) * TILE)
        kp_ref[sl, :] = jnp.dot(perm, k[sl].astype(BF16), preferred_element_type=F32).astype(BF16)
        vp = jnp.dot(perm, v[sl].astype(BF16), preferred_element_type=F32)
        vt_ref[half] = vp.T.astype(BF16)
        qt_ref[half] = (q[sl] * 0.5).T.astype(BF16)


def _inproj(h, gain, w, perm, layer, kv_bufs):
    row = lambda i: (i, 0)
    fixed = lambda i: (0, 0)
    tiles = lambda i: (i, 0, 0)
    lrow = lambda i: (layer, i, 0)
    per = TM // TILE
    kv_block = pl.BlockSpec((None, TM, 512), lrow)
    kv_shape = jax.ShapeDtypeStruct((DEPTH, ROWS, 512), F32)
    in_specs = [pl.BlockSpec((TM, D_MODEL), row),
                pl.BlockSpec((1, D_MODEL), fixed),
                pl.BlockSpec((D_MODEL, IN_WIDTH), fixed),
                pl.BlockSpec((TILE, TILE), fixed),
                pl.BlockSpec(memory_space=pl.ANY), pl.BlockSpec(memory_space=pl.ANY)]
    args = [h, gain, w, perm, *kv_bufs]
    aliases = {4: 2, 5: 3}
    return pl.pallas_call(
        _inproj_kernel,
        grid=(ROWS // TM,),
        in_specs=in_specs,
        out_specs=[pl.BlockSpec((TM, 512), row), pl.BlockSpec((TM, 512), row),
                   kv_block, kv_block,
                   pl.BlockSpec((TM, 512), row)]
        + [pl.BlockSpec((per, SB_WIDTH, TILE), tiles)] * 2,
        out_shape=[jax.ShapeDtypeStruct((ROWS, 512), F32),
                   jax.ShapeDtypeStruct((ROWS, 512), BF16),
                   kv_shape, kv_shape,
                   jax.ShapeDtypeStruct((ROWS, 512), BF16),
                   jax.ShapeDtypeStruct((ROWS // TILE, SB_WIDTH, TILE), BF16),
                   jax.ShapeDtypeStruct((ROWS // TILE, SB_WIDTH, TILE), BF16)],
        input_output_aliases=aliases,
        compiler_params=_params(("arbitrary",)),
        name="inproj",
    )(*args)


def _key_permutation():
    r = jnp.arange(TILE)
    src = (r % SUBLANES) * SEG + r // SUBLANES
    return (src[:, None] == jnp.arange(TILE)[None, :]).astype(BF16)


def _sb_half(zh, carry, lane0, masked):
    om = 0.5 - 0.5 * jnp.tanh(zh)
    if masked:
        r = lax.broadcasted_iota(I32, (TILE, LANES), 0)
        c = lax.broadcasted_iota(I32, (TILE, LANES), 1) + lane0
        key = ((r & (SUBLANES - 1)) * SEG) + (r >> 3)
        om = jnp.where(key < c, om, 1.0)
    run = jnp.ones((SUBLANES, LANES), F32)
    parts = [None] * SEG
    for a in reversed(range(SEG)):
        nxt = run * om[a * SUBLANES:(a + 1) * SUBLANES]
        parts[a] = run - nxt
        run = nxt
    sub = lax.broadcasted_iota(I32, (SUBLANES, LANES), 0)
    y = run
    for k in (1, 2, 4):
        y = y * jnp.where(sub + k < SUBLANES, pltpu.roll(y, SUBLANES - k, 0), 1.0)
    off = carry * jnp.where(sub + 1 < SUBLANES, pltpu.roll(y, SUBLANES - 1, 0), 1.0)
    new_carry = carry * jnp.broadcast_to(y[0:1, :], (SUBLANES, LANES))
    a_t = jnp.concatenate([p * off for p in parts], axis=0).astype(BF16)
    return a_t, new_carry


def _sb_tile(z, carry, masked):
    outs = []
    carries = []
    for lh in range(TILE // LANES):
        ls = slice(lh * LANES, (lh + 1) * LANES)
        a, c = _sb_half(z[:, ls], carry[:, ls], lh * LANES, masked)
        outs.append(a)
        carries.append(c)
    return jnp.concatenate(outs, axis=1), jnp.concatenate(carries, axis=1)


def _attn_prompt_kernel(nt_ref, k_ref, qt_ref, vt_ref, o_ref, zbuf, abuf):
    half = SB_HEAD_DIM
    row = lax.broadcasted_iota(I32, (LANES, TILE), 0)

    def k_tile(j):
        return k_ref[pl.ds(pl.multiple_of(j * TILE, TILE), TILE), :]

    def q_tile(i, _):
        qt = qt_ref[i].astype(F32)
        qh = (jnp.where(row < half, qt, 0.0).astype(BF16),
              jnp.where(row >= half, qt, 0.0).astype(BF16))
        ones = jnp.ones((SUBLANES, TILE), F32)

        kd = k_tile(i)
        carries = []
        for h in range(2):
            z = jnp.dot(kd, qh[h], preferred_element_type=F32)
            a, c = _sb_tile(z, ones, True)
            abuf[0, h] = a
            abuf[1, h] = jnp.zeros((TILE, TILE), BF16)
            carries.append(c)
        k0 = k_tile(jnp.maximum(i - 1, 0))
        k1 = k_tile(jnp.maximum(i - 2, 0))
        for h in range(2):
            zbuf[0, h] = jnp.dot(k0, qh[h], preferred_element_type=F32)
            zbuf[1, h] = jnp.dot(k1, qh[h], preferred_element_type=F32)

        def weighted_values(acc, v0, v1):
            out = []
            for h in range(2):
                rows = slice(h * half, (h + 1) * half)
                out.append(acc[h]
                           + jnp.dot(v0[rows], abuf[0, h], preferred_element_type=F32)
                           + jnp.dot(v1[rows], abuf[1, h], preferred_element_type=F32))
            return out

        def trip(st):
            t, _, c0, c1, acc0, acc1, p0, p1 = st
            j0 = i - 1 - 2 * t
            j1 = jnp.maximum(j0 - 1, 0)
            acc = weighted_values((acc0, acc1), vt_ref[p0], vt_ref[p1])
            cs = [c0, c1]
            for h in range(2):
                for s in range(2):
                    a, cs[h] = _sb_tile(zbuf[s, h], cs[h], False)
                    abuf[s, h] = a
            n0 = k_tile(jnp.maximum(j0 - 2, 0))
            n1 = k_tile(jnp.maximum(j0 - 3, 0))
            for h in range(2):
                zbuf[0, h] = jnp.dot(n0, qh[h], preferred_element_type=F32)
                zbuf[1, h] = jnp.dot(n1, qh[h], preferred_element_type=F32)
            alive = jnp.maximum(jnp.max(cs[0]), jnp.max(cs[1])) > 0.0
            return t + 1, alive, cs[0], cs[1], acc[0], acc[1], j0, j1

        zeros = jnp.zeros((half, TILE), F32)
        trips = (i + 1) // 2
        st = lax.while_loop(lambda s: (s[0] < trips) & s[1], trip,
                            (jnp.int32(0), jnp.bool_(True), carries[0], carries[1],
                             zeros, zeros, i, i))
        keep1 = (i == 0) | ((i & 1) == 0) | (st[0] < trips)
        v1 = jnp.where(keep1, vt_ref[st[7]].astype(F32), 0.0).astype(BF16)
        acc = weighted_values((st[4], st[5]), vt_ref[st[6]], v1)
        out_t = jnp.concatenate(acc, axis=0)
        o_ref[pl.ds(pl.multiple_of(i * TILE, TILE), TILE), :] = out_t.T.astype(BF16)
        return 0

    lax.fori_loop(0, nt_ref[0], q_tile, 0)


def _attn_prompt(kp, qt3, vt3, *, batch=BATCH, n_tiles=SEQ_TILES):
    rows = n_tiles * TILE
    return pl.pallas_call(
        _attn_prompt_kernel,
        grid=(batch, SB_HEADS // 2),
        in_specs=[pl.BlockSpec(memory_space=pltpu.SMEM),
                  pl.BlockSpec((rows, LANES), lambda p, g: (p, g)),
                  pl.BlockSpec((n_tiles, LANES, TILE), lambda p, g: (p, g, 0)),
                  pl.BlockSpec((n_tiles, LANES, TILE), lambda p, g: (p, g, 0))],
        out_specs=pl.BlockSpec((rows, LANES), lambda p, g: (p, g)),
        out_shape=jax.ShapeDtypeStruct((batch * rows, SB_WIDTH), BF16),
        scratch_shapes=[pltpu.VMEM((2, 2, TILE, TILE), F32),
                        pltpu.VMEM((2, 2, TILE, TILE), BF16)],
        compiler_params=_params(("arbitrary", "arbitrary")),
        name="attn_prompt",
    )(jnp.full((1,), n_tiles, I32), kp, qt3, vt3)


def _rev_excl_cumprod(om, scr):
    n = om.shape[0]
    sub = lax.broadcasted_iota(I32, (n, LANES), 0) & (SUBLANES - 1)
    scr[n:n + SUBLANES, :] = jnp.ones((SUBLANES, LANES), F32)
    y = om
    for k in (1, 2, 4):
        scr[0:n, :] = y
        y = y * jnp.where(sub + k < SUBLANES, scr[k:n + k, :], 1.0)
    scr[0:n, :] = y
    ex = jnp.where(sub + 1 < SUBLANES, scr[1:n + 1, :], 1.0)
    run = jnp.ones((SUBLANES, LANES), F32)
    out = [None] * (n // SUBLANES)
    for a in reversed(range(n // SUBLANES)):
        sl = slice(a * SUBLANES, (a + 1) * SUBLANES)
        out[a] = ex[sl] * run
        run = run * jnp.broadcast_to(y[a * SUBLANES:a * SUBLANES + 1, :], (SUBLANES, LANES))
    return jnp.concatenate(out, axis=0)


def _attn_sample_kernel(ck_ref, cv_ref, kn_ref, vn_ref, qbd_ref, o_ref, scr):
    qbd = qbd_ref[0]
    pad = jnp.zeros((LANES - DEC_SEQ, SB_WIDTH), F32)
    kn = jnp.concatenate([kn_ref[...], pad], axis=0).astype(BF16)
    vn = jnp.concatenate([vn_ref[...], pad], axis=0).astype(BF16)
    zc = jnp.dot(ck_ref[0].astype(BF16), qbd, preferred_element_type=F32)
    zn = jnp.dot(kn, qbd, preferred_element_type=F32)
    z = jnp.concatenate([zc, zn], axis=0)
    n = PAST_LEN + LANES
    r = lax.broadcasted_iota(I32, (n, LANES), 0)
    c = lax.broadcasted_iota(I32, (n, LANES), 1)
    valid = (r < PAST_LEN) | ((r - PAST_LEN) < (c & (DEC_SEQ - 1)))
    e = jnp.exp(z)
    om = 1.0 / (1.0 + e)
    beta = jnp.where(valid, 1.0 - om, 0.0)
    om = jnp.where(valid, om, 1.0)
    a_t = beta * _rev_excl_cumprod(om, scr)
    ac = a_t[0:PAST_LEN].T.astype(BF16)
    an = a_t[PAST_LEN:n].T.astype(BF16)
    p = jnp.dot(ac, cv_ref[0].astype(BF16), preferred_element_type=F32)
    p = p + jnp.dot(an, vn, preferred_element_type=F32)
    pr = lax.broadcasted_iota(I32, (LANES, SB_WIDTH), 0)
    pc = lax.broadcasted_iota(I32, (LANES, SB_WIDTH), 1)
    p = jnp.where((pr >> 4) == (pc >> 6), p, 0.0)
    out = p[0:DEC_SEQ]
    for h in range(1, SB_HEADS):
        out = out + p[h * DEC_SEQ:(h + 1) * DEC_SEQ]
    o_ref[...] = out.astype(BF16)


def _attn_sample(ck, cv, k, v, qbd, layer):
    new = lambda s: (layer, ROWS_P // DEC_SEQ + s, 0)
    per = lambda s: (s, 0, 0)
    cached = lambda s: (layer * DEC_BATCH + s, 0, 0)
    return pl.pallas_call(
        _attn_sample_kernel,
        grid=(DEC_BATCH,),
        in_specs=[pl.BlockSpec((1, PAST_LEN, SB_WIDTH), cached),
                  pl.BlockSpec((1, PAST_LEN, SB_WIDTH), cached),
                  pl.BlockSpec((None, DEC_SEQ, SB_WIDTH), new),
                  pl.BlockSpec((None, DEC_SEQ, SB_WIDTH), new),
                  pl.BlockSpec((1, SB_WIDTH, LANES), per)],
        out_specs=pl.BlockSpec((DEC_SEQ, SB_WIDTH), lambda s: (s, 0)),
        out_shape=jax.ShapeDtypeStruct((ROWS_S, SB_WIDTH), BF16),
        scratch_shapes=[pltpu.VMEM((PAST_LEN + LANES + SUBLANES, LANES), F32)],
        compiler_params=_params(("arbitrary",)),
        name="attn_sample",
    )(ck, cv, k, v, qbd)


def _pool_kernel(*refs, tm, use_halo):
    if use_halo:
        pv_ref, halo_ref, w_ref, sc_ref, o_ref, s1, s2, s4, s8 = refs
    else:
        pv_ref, w_ref, sc_ref, o_ref, s1, s2, s4, s8 = refs
    i = pl.program_id(0)
    off = SUBLANES
    n = HALO + tm
    x = pv_ref[...]
    zero8 = jnp.zeros((off, POOL_WIDTH), F32)
    if use_halo:
        first = (i % (SEQ_PAD // tm)) == 0
        halo = jnp.where(first, 0.0, halo_ref[...])
    else:
        halo = jnp.zeros((HALO, POOL_WIDTH), F32)
    s1[0:off, :] = zero8
    s1[off:off + HALO, :] = halo
    s1[off + HALO:off + n, :] = x
    g = POOL_GROUP_DIM
    t2 = s1[off:off + n, :] + s1[off - 1:off - 1 + n, :]
    s2[0:off, :] = zero8[:, g:]
    s2[off:off + n, :] = t2[:, g:]
    t4 = t2[:, g:] + s2[off - 2:off - 2 + n, :]
    s4[0:off, :] = zero8[:, 2 * g:]
    s4[off:off + n, :] = t4[:, g:]
    t8 = t4[:, g:] + s4[off - 4:off - 4 + n, :]
    s8[0:off, :] = zero8[:, 3 * g:]
    s8[off:off + n, :] = t8[:, g:]
    t16 = t8[:, g:] + s8[off - 8:off - 8 + n, :]
    sums = (t2[HALO:, 0:g], t4[HALO:, 0:g], t8[HALO:, 0:g], t16[HALO:, :])
    if use_halo:
        idx = (i % (SEQ_PAD // tm)) * tm - FRONT + lax.broadcasted_iota(I32, (tm, g), 0)
    for gi, win in enumerate(POOL_WINDOWS):
        if use_halo:
            cnt = jnp.clip(idx + 1, 1, win).astype(F32)
        else:
            cnt = jnp.full((tm, g), float(win), F32)
        d = sums[gi] / cnt - x[:, gi * g:(gi + 1) * g]
        y = jnp.dot(d.astype(BF16), w_ref[gi], preferred_element_type=F32)
        o_ref[:, gi * g:(gi + 1) * g] = (y * sc_ref[:, gi * g:(gi + 1) * g]).astype(BF16)


def _pool(pv, w, scale, *, rows, tm, use_halo):
    g = POOL_GROUP_DIM
    in_specs = [pl.BlockSpec((tm, POOL_WIDTH), lambda i: (i, 0))]
    args = [pv]
    if use_halo:
        in_specs.append(pl.BlockSpec((HALO, POOL_WIDTH),
                                     lambda i: (jnp.maximum(i * (tm // HALO) - 1, 0), 0)))
        args.append(pv)
    in_specs += [pl.BlockSpec((4, g, g), lambda i: (0, 0, 0)),
                 pl.BlockSpec((1, POOL_WIDTH), lambda i: (0, 0))]
    args += [w, scale]
    n = SUBLANES + HALO + tm
    return pl.pallas_call(
        functools.partial(_pool_kernel, tm=tm, use_halo=use_halo),
        grid=(rows // tm,),
        in_specs=in_specs,
        out_specs=pl.BlockSpec((tm, POOL_WIDTH), lambda i: (i, 0)),
        out_shape=jax.ShapeDtypeStruct((rows, POOL_WIDTH), BF16),
        scratch_shapes=[pltpu.VMEM((n, 4 * g), F32), pltpu.VMEM((n, 3 * g), F32),
                        pltpu.VMEM((n, 2 * g), F32), pltpu.VMEM((n, g), F32)],
        compiler_params=_params(("arbitrary",)),
        name="pool_prompt" if use_halo else "pool_sample",
    )(*args)


def _mix_kernel(*refs, route):
    if route:
        (h_ref, pp_ref, ps_ref, sp_ref, ss_ref, w_ref, g_ref, r_ref,
         hmid_ref, hn_ref, rt_ref) = refs
    else:
        h_ref, pp_ref, ps_ref, sp_ref, ss_ref, w_ref, g_ref, hmid_ref, hn_ref = refs
    prompt = pl.program_id(0) < ROWS_P // TM
    pool = jnp.where(prompt, pp_ref[...], ps_ref[...])
    sb = jnp.where(prompt, sp_ref[...], ss_ref[...])
    hmid = (h_ref[...]
            + jnp.dot(pool, w_ref[0:POOL_WIDTH, :], preferred_element_type=F32)
            + jnp.dot(sb, w_ref[POOL_WIDTH:, :], preferred_element_type=F32))
    hmid_ref[...] = hmid
    hn = _rms(hmid, g_ref[...])
    hn_ref[...] = hn.astype(hn_ref.dtype)
    if route:
        tm = hn.shape[0]
        h_hi = hn.astype(BF16)
        h_lo = (hn - h_hi.astype(F32)).astype(BF16)
        r = r_ref[...]
        r_hi = r.astype(BF16)
        r_lo = (r - r_hi.astype(F32)).astype(BF16)
        logits = (jnp.dot(h_hi, r_hi, preferred_element_type=F32)
                  + jnp.dot(h_hi, r_lo, preferred_element_type=F32)
                  + jnp.dot(h_lo, r_hi, preferred_element_type=F32))
        lane = lax.broadcasted_iota(I32, (tm, LANES), 1)
        neg = jnp.float32(-jnp.inf)
        logits = jnp.where(lane < N_EXPERTS, logits, neg)
        m1 = jnp.max(logits, axis=-1, keepdims=True)
        i1 = jnp.min(jnp.where(logits == m1, lane, LANES), axis=-1, keepdims=True)
        rest = jnp.where(lane == i1, neg, logits)
        m2 = jnp.max(rest, axis=-1, keepdims=True)
        i2 = jnp.min(jnp.where(rest == m2, lane, LANES), axis=-1, keepdims=True)
        t = jnp.exp(m2 - m1)
        g1 = 1.0 / (1.0 + t)
        g2 = t / (1.0 + t)
        rt_ref[...] = (jnp.where(lane == 0, i1.astype(F32), 0.0)
                       + jnp.where(lane == 1, i2.astype(F32), 0.0)
                       + jnp.where(lane == 2, g1, 0.0) + jnp.where(lane == 3, g2, 0.0))


def _mix(h, pool_p, pool_s, sb_p, sb_s, w, gain, router=None):
    row = lambda i: (i, 0)
    fixed = lambda i: (0, 0)
    prow = lambda i: (jnp.minimum(i, ROWS_P // TM - 1), 0)
    route = router is not None
    in_specs = [pl.BlockSpec((TM, D_MODEL), row),
                pl.BlockSpec((TM, POOL_WIDTH), prow),
                pl.BlockSpec((TM, POOL_WIDTH), fixed),
                pl.BlockSpec((TM, SB_WIDTH), prow),
                pl.BlockSpec((TM, SB_WIDTH), fixed),
                pl.BlockSpec((D_MODEL, D_MODEL), fixed),
                pl.BlockSpec((1, D_MODEL), fixed)]
    out_specs = [pl.BlockSpec((TM, D_MODEL), row), pl.BlockSpec((TM, D_MODEL), row)]
    out_shape = [jax.ShapeDtypeStruct((ROWS, D_MODEL), F32),
                 jax.ShapeDtypeStruct((ROWS, D_MODEL), F32 if route else BF16)]
    args = [h, pool_p, pool_s, sb_p, sb_s, w, gain]
    if route:
        in_specs.append(pl.BlockSpec((D_MODEL, LANES), fixed))
        out_specs.append(pl.BlockSpec((TM, LANES), row))
        out_shape.append(jax.ShapeDtypeStruct((ROWS, LANES), F32))
        args.append(router)
    return pl.pallas_call(
        functools.partial(_mix_kernel, route=route),
        grid=(ROWS // TM,),
        in_specs=in_specs, out_specs=out_specs, out_shape=out_shape,
        compiler_params=_params(("arbitrary",)),
        name="mix_route" if route else "mix",
    )(*args)


def _silu_mul(g, u):
    return g * (1.0 / (1.0 + jnp.exp(-g))) * u


FF_CHUNK = 256


def _ffn_kernel(hmid_ref, hn_ref, wg_ref, wu_ref, wd_ref, o_ref, act):
    hn = hn_ref[...]
    for c in range(D_FF // FF_CHUNK):
        sl = slice(c * FF_CHUNK, (c + 1) * FF_CHUNK)
        g = jnp.dot(hn, wg_ref[:, sl], preferred_element_type=F32)
        u = jnp.dot(hn, wu_ref[:, sl], preferred_element_type=F32)
        act[:, sl] = _silu_mul(g, u).astype(BF16)
    o_ref[...] = hmid_ref[...] + jnp.dot(act[...], wd_ref[...], preferred_element_type=F32)


def _ffn(hmid, hn, wg, wu, wd):
    row = lambda i: (i, 0)
    fixed = lambda i: (0, 0)
    once = pl.Buffered(1)
    return pl.pallas_call(
        _ffn_kernel,
        grid=(ROWS // TM,),
        in_specs=[pl.BlockSpec((TM, D_MODEL), row),
                  pl.BlockSpec((TM, D_MODEL), row),
                  pl.BlockSpec((D_MODEL, D_FF), fixed, pipeline_mode=once),
                  pl.BlockSpec((D_MODEL, D_FF), fixed, pipeline_mode=once),
                  pl.BlockSpec((D_FF, D_MODEL), fixed, pipeline_mode=once)],
        out_specs=pl.BlockSpec((TM, D_MODEL), row),
        out_shape=jax.ShapeDtypeStruct((ROWS, D_MODEL), F32),
        scratch_shapes=[pltpu.VMEM((TM, D_FF), BF16)],
        compiler_params=_params(("arbitrary",)),
        name="ffn",
    )(hmid, hn, wg, wu, wd)


def _route_plan(route):
    e_flat = jnp.concatenate([route[:, 0], route[:, 1]]).astype(I32)
    order = jnp.argsort(e_flat).astype(I32)
    ex = jnp.arange(N_EXPERTS, dtype=I32)
    counts = jnp.sum((e_flat[:, None] == ex[None, :]).astype(I32), axis=0)
    gstart = jnp.cumsum(counts) - counts
    padded = ((counts + TM_E - 1) // TM_E) * TM_E
    pend = jnp.cumsum(padded)
    pstart = pend - padded
    q = jnp.arange(P_MAX, dtype=I32)
    e_q = jnp.sum((q[:, None] >= pend[None, :]).astype(I32), axis=1)
    e_c = jnp.minimum(e_q, N_EXPERTS - 1)
    within = q - pstart[e_c]
    valid = (e_q < N_EXPERTS) & (within < counts[e_c])
    pair = order[jnp.clip(gstart[e_c] + within, 0, PAIRS - 1)]
    src = jnp.where(valid, jnp.where(pair >= ROWS, pair - ROWS, pair), 0)
    spare = PAIRS + ((q // TM_E) % 2) * TM_E + (q % TM_E)
    dst = jnp.where(valid, pair, spare)
    tile_expert = e_c[::TM_E]
    n_used = (pend[-1] // TM_E).reshape(1)
    return (src.reshape(NT_E, 1, TM_E), dst.reshape(NT_E, 1, TM_E), tile_expert, n_used)


DMA_UNROLL = 8


def _row_dmas(per, make):
    def issue(r8, _):
        for u in range(DMA_UNROLL):
            make(r8 * DMA_UNROLL + u).start(priority=u % 2)
        return 0

    lax.fori_loop(0, per // DMA_UNROLL, issue, 0)


def _experts_kernel(te_ref, nu_ref, src0_ref, srcn_ref, dstp_ref, dst_ref, hn_ref, wg_ref, wu_ref,
                    wd_ref, slots_ref, xbuf, ybuf, act, gsem, ssem):
    t = pl.program_id(0)
    c = pl.program_id(1)
    n_used = nu_ref[0]
    used = t < n_used
    slot = t % 2
    n_sub = EX_CHUNK // FF_CHUNK
    per_sub = -(-TM_E // n_sub)

    def fetch_row(idx_ref, s, r):
        return pltpu.make_async_copy(hn_ref.at[pl.ds(idx_ref[0, 0, r], 1)],
                                     xbuf.at[s, pl.ds(r, 1)], gsem.at[s])

    def fetch_wait(s):
        pltpu.make_async_copy(hn_ref.at[pl.ds(0, TM_E)], xbuf.at[s], gsem.at[s]).wait()

    def send_row(idx_ref, s, r):
        return pltpu.make_async_copy(ybuf.at[s, pl.ds(r, 1)],
                                     slots_ref.at[pl.ds(idx_ref[0, 0, r], 1)], ssem.at[s])

    def send_wait(s):
        pltpu.make_async_copy(ybuf.at[s], slots_ref.at[pl.ds(0, TM_E)], ssem.at[s]).wait()

    def compute(row_dma):
        x = xbuf[slot].astype(BF16)
        for s in range(n_sub):
            sl = slice(s * FF_CHUNK, (s + 1) * FF_CHUNK)
            g = jnp.dot(x, wg_ref[0, :, sl], preferred_element_type=F32)
            u = jnp.dot(x, wu_ref[0, :, sl], preferred_element_type=F32)
            act[:, sl] = _silu_mul(g, u).astype(BF16)
            if row_dma is not None:
                for r in range(s * per_sub, min((s + 1) * per_sub, TM_E)):
                    row_dma(r).start(priority=r % 2)
        return jnp.dot(act[...], wd_ref[0], preferred_element_type=F32)

    @pl.when((t == 0) & (c == 0))
    def _():
        ybuf[0] = jnp.zeros((TM_E, D_MODEL), F32)
        spare = [pltpu.make_async_copy(ybuf.at[0], slots_ref.at[pl.ds(PAIRS + k * TM_E, TM_E)],
                                       ssem.at[0])
                 for k in range((SLOT_ROWS - PAIRS) // TM_E)]
        for cp in spare:
            cp.start()
        for cp in spare:
            cp.wait()
        _row_dmas(TM_E, lambda r: fetch_row(src0_ref, 0, r))

    @pl.when(used & (c == 0))
    def _():
        fetch_wait(slot)

        @pl.when(t >= 2)
        def _():
            send_wait(slot)

    @pl.when(used & (c == 0))
    def _():
        ybuf[slot] = compute(lambda r: fetch_row(srcn_ref, 1 - slot, r))

    @pl.when(used & (c == 1) & (t >= 1))
    def _():
        ybuf[slot] += compute(lambda r: send_row(dstp_ref, 1 - slot, r))

    @pl.when(used & (c == 1) & (t == 0))
    def _():
        ybuf[slot] += compute(None)

    @pl.when((t == n_used - 1) & (c == 1))
    def _():
        _row_dmas(TM_E, lambda r: send_row(dst_ref, slot, r))

    @pl.when((t == pl.num_programs(0) - 1) & (c == 1))
    def _():
        send_wait((n_used - 1) % 2)
        fetch_wait(n_used % 2)

        @pl.when(n_used >= 2)
        def _():
            send_wait(n_used % 2)


def _experts(tile_expert, n_used, src, dst, hn, wg, wu, wd):
    n_chunks = D_EXPERT // EX_CHUNK
    assert n_chunks == 2

    def held(t, c, nu):
        live = t < nu[0]
        return jnp.where(live, t, nu[0] - 1), jnp.where(live, c, n_chunks - 1)

    def w_in_map(t, c, te, nu):
        tt, cc = held(t, c, nu)
        return te[tt], 0, cc

    def w_out_map(t, c, te, nu):
        tt, cc = held(t, c, nu)
        return te[tt], cc, 0

    idx_block = lambda fn: pl.BlockSpec((1, 1, TM_E), fn, memory_space=pltpu.SMEM)
    grid_spec = pltpu.PrefetchScalarGridSpec(
        num_scalar_prefetch=2,
        grid=(NT_E, n_chunks),
        in_specs=[idx_block(lambda t, c, te, nu: (0, 0, 0)),
                  idx_block(lambda t, c, te, nu: (jnp.minimum(t + 1, NT_E - 1), 0, 0)),
                  idx_block(lambda t, c, te, nu: (jnp.maximum(t - 1, 0), 0, 0)),
                  idx_block(lambda t, c, te, nu: (t, 0, 0)),
                  pl.BlockSpec(memory_space=pl.ANY),
                  pl.BlockSpec((1, D_MODEL, EX_CHUNK), w_in_map),
                  pl.BlockSpec((1, D_MODEL, EX_CHUNK), w_in_map),
                  pl.BlockSpec((1, EX_CHUNK, D_MODEL), w_out_map)],
        out_specs=pl.BlockSpec(memory_space=pl.ANY),
        scratch_shapes=[pltpu.VMEM((2, TM_E, D_MODEL), F32), pltpu.VMEM((2, TM_E, D_MODEL), F32),
                        pltpu.VMEM((TM_E, EX_CHUNK), BF16),
                        pltpu.SemaphoreType.DMA((2,)), pltpu.SemaphoreType.DMA((2,))])
    return pl.pallas_call(
        _experts_kernel,
        grid_spec=grid_spec,
        out_shape=jax.ShapeDtypeStruct((SLOT_ROWS, D_MODEL), F32),
        compiler_params=_params(("arbitrary", "arbitrary")),
        name="experts",
    )(tile_expert, n_used, src, src, dst, dst, hn, wg, wu, wd)


def _combine_kernel(hmid_ref, a_ref, b_ref, rt_ref, fg_ref, yp_ref, ys_ref):
    i = pl.program_id(0)
    lane = lax.broadcasted_iota(I32, rt_ref.shape, 1)
    rt = rt_ref[...]
    g1 = jnp.sum(jnp.where(lane == 2, rt, 0.0), axis=-1, keepdims=True)
    g2 = jnp.sum(jnp.where(lane == 3, rt, 0.0), axis=-1, keepdims=True)
    y = _rms(hmid_ref[...] + g1 * a_ref[...] + g2 * b_ref[...], fg_ref[...])
    prompt = i < ROWS_P // TILE

    @pl.when(prompt & (i % SEQ_TILES != 0))
    def _():
        yp_ref[0] = y

    @pl.when(jnp.logical_not(prompt))
    def _():
        ys_ref[...] = y


def _combine(hmid, slots, route, fgain):
    n_p = ROWS_P // TILE

    def yp_map(i):
        ip = jnp.minimum(i, n_p - 1)
        return (ip // SEQ_TILES, jnp.maximum(ip % SEQ_TILES - 1, 0), 0)

    return pl.pallas_call(
        _combine_kernel,
        grid=(ROWS // TILE,),
        in_specs=[pl.BlockSpec((TILE, D_MODEL), lambda i: (i, 0)),
                  pl.BlockSpec((TILE, D_MODEL), lambda i: (i, 0)),
                  pl.BlockSpec((TILE, D_MODEL), lambda i: (i + ROWS // TILE, 0)),
                  pl.BlockSpec((TILE, LANES), lambda i: (i, 0)),
                  pl.BlockSpec((1, D_MODEL), lambda i: (0, 0))],
        out_specs=[pl.BlockSpec((1, TILE, D_MODEL), yp_map),
                   pl.BlockSpec((TILE, D_MODEL), lambda i: (jnp.maximum(i - n_p, 0), 0))],
        out_shape=[jax.ShapeDtypeStruct((BATCH, SEQ, D_MODEL), F32),
                   jax.ShapeDtypeStruct((ROWS_S, D_MODEL), F32)],
        compiler_params=_params(("arbitrary",)),
        name="combine",
    )(hmid, slots, slots, route, fgain)


def _moe(hmid, hn, route, wg, wu, wd, fgain):
    src, dst, tile_expert, n_used = _route_plan(route)
    slots = _experts(tile_expert, n_used, src, dst, hn, wg, wu, wd)
    return _combine(hmid, slots, route, fgain)


def _sample_qbd(q):
    qs = q[ROWS_P:].reshape(DEC_BATCH, DEC_SEQ, SB_HEADS, SB_HEAD_DIM)
    eye = jnp.eye(SB_HEADS, dtype=q.dtype)
    qbd = jnp.einsum('sthd,hg->shdgt', qs, eye)
    return qbd.reshape(DEC_BATCH, SB_WIDTH, SB_HEADS * DEC_SEQ)


def kernel(x_prompt, x_sample, cache_k, cache_v, state_pool, meta_tokens, norm_mix, w_in, pool_w, pool_scale, w_out, norm_ffn, ffn_w_gate, ffn_w_up, ffn_w_down, moe_router, moe_w_gate, moe_w_up, moe_w_down, final_norm):
    h = _token_rows(x_prompt, meta_tokens, x_sample.reshape(ROWS_S, D_MODEL))
    perm = _key_permutation()
    ck_all = cache_k.reshape(DEPTH * DEC_BATCH, PAST_LEN, SB_WIDTH)
    cv_all = cache_v.reshape(DEPTH * DEC_BATCH, PAST_LEN, SB_WIDTH)

    outs = {name: [] for name in ("pp", "sp")}
    y_prompt = y_sample = None
    kv_bufs = (jnp.zeros((DEPTH, ROWS, 512), F32), jnp.zeros((DEPTH, ROWS, 512), F32))
    for l in range(DEPTH):
        pv, q, k, v, kp, qt3, vt3 = _inproj(h, norm_mix[l][None], w_in[l].astype(BF16), perm,
                                            l, kv_bufs)
        kv_bufs = (k, v)

        sb_p = _attn_prompt(kp, qt3, vt3)
        sb_s = _attn_sample(ck_all, cv_all, k, v, _sample_qbd(q), l)

        pw = pool_w[l].astype(BF16)
        ps = pool_scale[l][None]
        pool_p = _pool(pv, pw, ps, rows=ROWS_P, tm=TILE, use_halo=True)
        pv_s = pv[ROWS_P:].reshape(DEC_BATCH, DEC_SEQ, POOL_WIDTH)
        pbuf = jnp.concatenate([jnp.zeros((DEC_BATCH, 1, POOL_WIDTH), F32), state_pool[l], pv_s],
                               axis=1).reshape(DEC_BATCH * 2 * DEC_SEQ, POOL_WIDTH)
        pool_s = _pool(pbuf, pw, ps, rows=DEC_BATCH * 2 * DEC_SEQ, tm=TM, use_halo=False)
        pool_s = pool_s.reshape(DEC_BATCH, 2 * DEC_SEQ, POOL_WIDTH)[:, DEC_SEQ:].reshape(ROWS_S, POOL_WIDTH)

        wo = w_out[l].astype(BF16)
        j = l // 2
        if l % 2 == 0:
            hmid, hn = _mix(h, pool_p, pool_s, sb_p, sb_s, wo, norm_ffn[l][None])
            h = _ffn(hmid, hn, ffn_w_gate[j].astype(BF16), ffn_w_up[j].astype(BF16),
                     ffn_w_down[j].astype(BF16))
        else:
            router = jnp.pad(moe_router[j], ((0, 0), (0, LANES - N_EXPERTS)))
            hmid, hn, route = _mix(h, pool_p, pool_s, sb_p, sb_s, wo, norm_ffn[l][None], router)
            y_prompt, y_sample = _moe(hmid, hn, route, moe_w_gate[j].astype(BF16),
                                      moe_w_up[j].astype(BF16), moe_w_down[j].astype(BF16),
                                      final_norm[None])

        outs["pp"].append(jnp.stack([pv[(p + 1) * SEQ_PAD - POOL_STATE:(p + 1) * SEQ_PAD]
                                     for p in range(BATCH)]))
        outs["sp"].append(pv_s[:, -POOL_STATE:])

    def prompt_rows(buf):
        rows = buf[:, :ROWS_P].reshape(DEPTH, BATCH, SEQ_PAD, SB_HEADS, SB_HEAD_DIM)
        return rows[:, :, FRONT:]

    def sample_rows(buf):
        return buf[:, ROWS_P:].reshape(DEPTH, DEC_BATCH, DEC_SEQ, SB_HEADS, SB_HEAD_DIM)

    k, v = kv_bufs
    y_sample = y_sample.reshape(DEC_BATCH, DEC_SEQ, D_MODEL)
    return (y_prompt, y_sample, prompt_rows(k), prompt_rows(v), jnp.stack(outs["pp"]),
            sample_rows(k), sample_rows(v), jnp.stack(outs["sp"]))
```
